```python
import functools
import jax, jax.numpy as jnp
from jax import lax
import numpy as np

D_MODEL = 1024
BATCH = 4
SEQ = 4096
DEPTH = 2
DEC_BATCH = 8
DEC_SEQ = 64
PAST_LEN = 4096

CHUNK = 64
N_HEADS = 16
HEAD_DIM = 64
D_ATT = N_HEADS * HEAD_DIM
D_CONV = D_MODEL
CONV_W = 31
N_PAST_CHUNKS = 8
ATT_PAST = N_PAST_CHUNKS * CHUNK
REL_CLIP = 128
D_FF = 2816
FFN_W = 3
PLE_DIM = 256
N_IN = 2 * D_CONV + 3 * D_ATT + 2 * D_MODEL
SPLITS = [D_CONV, 2 * D_CONV, 2 * D_CONV + D_ATT, 2 * D_CONV + 2 * D_ATT, 2 * D_CONV + 3 * D_ATT]
EPS = 1e-6
NEG_INF = -1e30

kernel_name = "streaming_conformer_hybrid_step"


def rms_norm(x, g):
    xf = x.astype(jnp.float32)
    y = xf * lax.rsqrt(jnp.mean(xf * xf, axis=-1, keepdims=True) + EPS)
    return (y * g.astype(jnp.float32)).astype(x.dtype)


def layer_norm(x, g, b):
    xf = x.astype(jnp.float32)
    mu = jnp.mean(xf, axis=-1, keepdims=True)
    var = jnp.mean(jnp.square(xf - mu), axis=-1, keepdims=True)
    y = (xf - mu) * lax.rsqrt(var + EPS)
    return (y * g.astype(jnp.float32) + b.astype(jnp.float32)).astype(x.dtype)


def causal_dwconv(x, hist, w, b):
    xe = jnp.concatenate([hist, x], axis=1)
    width, c = w.shape
    y = lax.conv_general_dilated(xe, w[:, None, :], window_strides=(1,), padding="VALID",
                                 dimension_numbers=("NWC", "WIO", "NWC"), feature_group_count=c)
    return y + b, xe[:, xe.shape[1] - (width - 1):]


def rel_bias(table, dist):
    return table[:, jnp.clip(dist, -REL_CLIP, REL_CLIP) + REL_CLIP].astype(jnp.float32)


def band_attention_prompt(q, k, v, rel_table):
    b, s, h, dh = q.shape
    nc = s // CHUNK
    n_band = N_PAST_CHUNKS + 1
    pad = ((0, 0), (ATT_PAST, 0), (0, 0), (0, 0))
    kc = jnp.pad(k, pad).reshape(b, nc + N_PAST_CHUNKS, CHUNK, h, dh)
    vc = jnp.pad(v, pad).reshape(b, nc + N_PAST_CHUNKS, CHUNK, h, dh)
    band_k = jnp.concatenate([kc[:, o:o + nc] for o in range(n_band)], axis=2)
    band_v = jnp.concatenate([vc[:, o:o + nc] for o in range(n_band)], axis=2)
    qc = q.reshape(b, nc, CHUNK, h, dh)
    sc = jnp.einsum("bnqhd,bnkhd->bnhqk", qc, band_k).astype(jnp.float32) * (dh ** -0.5)
    kj = jnp.arange(n_band * CHUNK)
    sc = sc + rel_bias(rel_table, jnp.arange(CHUNK)[:, None] + ATT_PAST - kj[None, :])[None, None]
    kpos = jnp.arange(nc)[:, None] * CHUNK - ATT_PAST + kj[None, :]
    sc = jnp.where((kpos >= 0)[None, :, None, None, :], sc, NEG_INF)
    pr = jax.nn.softmax(sc, axis=-1).astype(v.dtype)
    o = jnp.einsum("bnhqk,bnkhd->bnqhd", pr, band_v).reshape(b, s, h * dh)
    keep = min(ATT_PAST, s)
    return o, k[:, s - keep:], v[:, s - keep:]


def band_attention_sample(q, k, v, rel_table, cache_k, cache_v):
    b, t, h, dh = q.shape
    l = cache_k.shape[1]
    ks = jnp.concatenate([cache_k, k], axis=1)
    vs = jnp.concatenate([cache_v, v], axis=1)
    sc = jnp.einsum("bqhd,bkhd->bhqk", q, ks).astype(jnp.float32) * (dh ** -0.5)
    sc = sc + rel_bias(rel_table, jnp.arange(t)[:, None] + l - jnp.arange(l + t)[None, :])[None]
    pr = jax.nn.softmax(sc, axis=-1).astype(v.dtype)
    o = jnp.einsum("bhqk,bkhd->bqhd", pr, vs).reshape(b, t, h * dh)
    return o, ks[:, t:], vs[:, t:]


def trunk_layer(x, p, conv_hist, ffn_hist, attend, prm):
    (norm_mix, w_in, conv_dw, conv_dw_b, conv_ln_g, conv_ln_b, w_conv_out, rel_table, w_att_out,
     b_gate, w_out, norm_ffn, w_ffn_up, ffn_dw, ffn_dw_b, w_ffn_down, norm_ple, w_ple_gate, w_ple_proj) = prm
    b, t, _ = x.shape
    h = rms_norm(x, norm_mix)
    glu_a, glu_b, q, k, v, gates = jnp.split(h @ w_in, SPLITS, axis=-1)
    u = glu_a * jax.nn.sigmoid(glu_b)
    c, conv_state = causal_dwconv(u, conv_hist, conv_dw, conv_dw_b)
    c = jax.nn.silu(layer_norm(c, conv_ln_g, conv_ln_b)) @ w_conv_out
    shp = (b, t, N_HEADS, HEAD_DIM)
    a, k_state, v_state = attend(q.reshape(shp), k.reshape(shp), v.reshape(shp), rel_table)
    a = a @ w_att_out
    g = jax.nn.sigmoid(gates + b_gate)
    x = x + (g[..., :D_MODEL] * c + g[..., D_MODEL:] * a) @ w_out
    h = rms_norm(x, norm_ffn)
    up, gate = jnp.split(h @ w_ffn_up, 2, axis=-1)
    up, ffn_state = causal_dwconv(up, ffn_hist, ffn_dw, ffn_dw_b)
    x = x + (jax.nn.gelu(up, approximate=False) * gate) @ w_ffn_down
    h = rms_norm(x, norm_ple)
    x = x + jax.nn.sigmoid(h @ w_ple_gate) * (p @ w_ple_proj)
    return x, k_state, v_state, conv_state, ffn_state


def setup_inputs(seed: int = 0) -> dict:
    key = jax.random.key(seed)
    ks = iter(jax.random.split(key, 40))

    def nrm(shape, scale=1.0):
        return jax.random.normal(next(ks), shape, jnp.float32) * scale

    def gain(shape):
        return 1.0 + nrm(shape, 0.01)

    l_cache = min(ATT_PAST, PAST_LEN)
    return {
        "x_prompt": nrm((BATCH, SEQ, D_MODEL)),
        "x_sample": nrm((DEC_BATCH, DEC_SEQ, D_MODEL)),
        "p_prompt": nrm((DEPTH, BATCH, SEQ, PLE_DIM)),
        "p_sample": nrm((DEPTH, DEC_BATCH, DEC_SEQ, PLE_DIM)),
        "cache_att_k": nrm((DEPTH, DEC_BATCH, l_cache, N_HEADS, HEAD_DIM)),
        "cache_att_v": nrm((DEPTH, DEC_BATCH, l_cache, N_HEADS, HEAD_DIM)),
        "state_conv": nrm((DEPTH, DEC_BATCH, CONV_W - 1, D_CONV), 0.5),
        "state_ffn_conv": nrm((DEPTH, DEC_BATCH, FFN_W - 1, D_FF)),
        "norm_mix": gain((DEPTH, D_MODEL)),
        "w_in": nrm((DEPTH, D_MODEL, N_IN), D_MODEL ** -0.5),
        "conv_dw": nrm((DEPTH, CONV_W, D_CONV), CONV_W ** -0.5),
        "conv_dw_b": nrm((DEPTH, D_CONV), 0.01),
        "conv_ln_g": gain((DEPTH, D_CONV)),
        "conv_ln_b": nrm((DEPTH, D_CONV), 0.01),
        "w_conv_out": nrm((DEPTH, D_CONV, D_MODEL), D_CONV ** -0.5),
        "rel_table": nrm((DEPTH, N_HEADS, 2 * REL_CLIP + 1), 0.1),
        "w_att_out": nrm((DEPTH, D_ATT, D_MODEL), D_ATT ** -0.5),
        "b_gate": nrm((DEPTH, 2 * D_MODEL), 0.01),
        "w_out": nrm((DEPTH, D_MODEL, D_MODEL), D_MODEL ** -0.5),
        "norm_ffn": gain((DEPTH, D_MODEL)),
        "w_ffn_up": nrm((DEPTH, D_MODEL, 2 * D_FF), D_MODEL ** -0.5),
        "ffn_dw": nrm((DEPTH, FFN_W, D_FF), FFN_W ** -0.5),
        "ffn_dw_b": nrm((DEPTH, D_FF), 0.01),
        "w_ffn_down": nrm((DEPTH, D_FF, D_MODEL), D_FF ** -0.5),
        "norm_ple": gain((DEPTH, D_MODEL)),
        "w_ple_gate": nrm((DEPTH, D_MODEL, D_MODEL), D_MODEL ** -0.5),
        "w_ple_proj": nrm((DEPTH, PLE_DIM, D_MODEL), PLE_DIM ** -0.5),
        "norm_final": gain((D_MODEL,)),
    }


def reference(x_prompt, x_sample, p_prompt, p_sample, cache_att_k, cache_att_v, state_conv, state_ffn_conv,
              norm_mix, w_in, conv_dw, conv_dw_b, conv_ln_g, conv_ln_b, w_conv_out, rel_table, w_att_out,
              b_gate, w_out, norm_ffn, w_ffn_up, ffn_dw, ffn_dw_b, w_ffn_down, norm_ple, w_ple_gate,
              w_ple_proj, norm_final):
    xp, xs = x_prompt, x_sample
    bp = xp.shape[0]
    zero_conv = jnp.zeros((bp, CONV_W - 1, D_CONV), xp.dtype)
    zero_ffn = jnp.zeros((bp, FFN_W - 1, D_FF), xp.dtype)
    kp_l, vp_l, cp_l, fp_l, ks_l, vs_l, cs_l, fs_l = [], [], [], [], [], [], [], []
    for i in range(DEPTH):
        prm = (norm_mix[i], w_in[i], conv_dw[i], conv_dw_b[i], conv_ln_g[i], conv_ln_b[i], w_conv_out[i],
               rel_table[i], w_att_out[i], b_gate[i], w_out[i], norm_ffn[i], w_ffn_up[i], ffn_dw[i],
               ffn_dw_b[i], w_ffn_down[i], norm_ple[i], w_ple_gate[i], w_ple_proj[i])
        xp, kp, vp, cp, fp = trunk_layer(xp, p_prompt[i], zero_conv, zero_ffn, band_attention_prompt, prm)
        attend_s = functools.partial(band_attention_sample, cache_k=cache_att_k[i], cache_v=cache_att_v[i])
        xs, ksn, vsn, csn, fsn = trunk_layer(xs, p_sample[i], state_conv[i], state_ffn_conv[i], attend_s, prm)
        kp_l.append(kp); vp_l.append(vp); cp_l.append(cp); fp_l.append(fp)
        ks_l.append(ksn); vs_l.append(vsn); cs_l.append(csn); fs_l.append(fsn)
    y_prompt = rms_norm(xp, norm_final)
    y_sample = rms_norm(xs, norm_final)
    new_k_prompt = jnp.stack(kp_l)
    new_v_prompt = jnp.stack(vp_l)
    new_conv_prompt = jnp.stack(cp_l)
    new_ffn_prompt = jnp.stack(fp_l)
    new_k_sample = jnp.stack(ks_l)
    new_v_sample = jnp.stack(vs_l)
    new_conv_sample = jnp.stack(cs_l)
    new_ffn_sample = jnp.stack(fs_l)
    return (y_prompt, y_sample, new_k_prompt, new_v_prompt, new_conv_prompt, new_ffn_prompt,
            new_k_sample, new_v_sample, new_conv_sample, new_ffn_sample)
```

```python
import functools

import jax
import jax.numpy as jnp
from jax import lax
from jax.experimental import pallas as pl
from jax.experimental.pallas import tpu as pltpu

D_MODEL = 1024
N_HEADS = 16
HEAD_DIM = 64
CHUNK = 64
N_PAST_CHUNKS = 8
ATT_PAST = N_PAST_CHUNKS * CHUNK
REL_CLIP = 128
CONV_W = 31
D_FF = 2816
FFN_W = 3
PLE_DIM = 256
EPS = 1e-6
NEG_INF = -1e30

LANES = 128
SUBLANES = 8
N_PAIRS = D_MODEL // LANES
Q_BLOCK = 2 * CHUNK
K_WINDOW = ATT_PAST + Q_BLOCK
CONV_HIST = 32
FFN_HIST = 8
VMEM_LIMIT = 60 * 1024 * 1024

BF16 = jnp.bfloat16
F32 = jnp.float32


def _const_spec(shape):
    zeros = (0,) * len(shape)
    return pl.BlockSpec(shape, lambda *_: zeros, pipeline_mode=pl.Buffered(1))


def _params(n_axes):
    return pltpu.CompilerParams(dimension_semantics=("arbitrary",) * n_axes, vmem_limit_bytes=VMEM_LIMIT)


def _rms(x, w):
    return x * lax.rsqrt(jnp.mean(x * x, axis=-1, keepdims=True) + EPS) * w


def _dot(a, b):
    return jnp.dot(a, b, preferred_element_type=F32)


def _in_proj_kernel(x_ref, nw_ref, w_ref, bg_ref, u_ref, q_ref, k_ref, v_ref, g_ref, kst_ref, vst_ref):
    d = D_MODEL
    h = _rms(x_ref[...], nw_ref[...]).astype(BF16)
    glu_a = _dot(h, w_ref[:, 0:d])
    glu_b = _dot(h, w_ref[:, d:2 * d])
    u_ref[...] = glu_a * jax.nn.sigmoid(glu_b)
    q_ref[...] = (_dot(h, w_ref[:, 2 * d:3 * d]) * (HEAD_DIM ** -0.5)).astype(BF16)
    k = _dot(h, w_ref[:, 3 * d:4 * d])
    k_ref[...] = k.astype(BF16)
    kst_ref[0] = k
    v = _dot(h, w_ref[:, 4 * d:5 * d])
    v_ref[...] = v.astype(BF16)
    vst_ref[0] = v
    g_ref[...] = jax.nn.sigmoid(_dot(h, w_ref[:, 5 * d:7 * d]) + bg_ref[...])


def _in_proj(x, norm_w, w_in, b_gate, tiles_per_seq, tm=512):
    t, d = x.shape
    n_tiles = t // tm
    n_seq = n_tiles // tiles_per_seq
    row = lambda i: (i, 0)
    st = lambda i: (i // tiles_per_seq, 0, 0)
    return pl.pallas_call(
        _in_proj_kernel,
        grid=(n_tiles,),
        in_specs=[pl.BlockSpec((tm, d), row), _const_spec((1, d)), _const_spec((d, 7 * d)), _const_spec((1, 2 * d))],
        out_specs=[pl.BlockSpec((tm, d), row), pl.BlockSpec((tm, d), row), pl.BlockSpec((tm, d), row),
                   pl.BlockSpec((tm, d), row), pl.BlockSpec((tm, 2 * d), row),
                   pl.BlockSpec((1, tm, d), st), pl.BlockSpec((1, tm, d), st)],
        out_shape=[jax.ShapeDtypeStruct((t, d), F32), jax.ShapeDtypeStruct((t, d), BF16),
                   jax.ShapeDtypeStruct((t, d), BF16), jax.ShapeDtypeStruct((t, d), BF16),
                   jax.ShapeDtypeStruct((t, 2 * d), F32),
                   jax.ShapeDtypeStruct((n_seq, tm, d), F32), jax.ShapeDtypeStruct((n_seq, tm, d), F32)],
        compiler_params=_params(1),
        name="in_proj",
    )(x, norm_w, w_in, b_gate)


def _conv_kernel(u_ref, hist_ref, w_ref, b_ref, lg_ref, lb_ref, wo_ref, c_ref, xe_ref, xs_ref, y_ref, *, tm, rows):
    j = pl.program_id(1)

    @pl.when(j == 0)
    def _():
        xe_ref[0:CONV_HIST] = hist_ref[0]

    @pl.when(j > 0)
    def _():
        xe_ref[0:CONV_HIST] = xe_ref[tm:tm + CONV_HIST]

    xe_ref[CONV_HIST:CONV_HIST + tm] = u_ref[...]
    n_shift = CONV_HIST + tm - SUBLANES
    for r in range(1, SUBLANES):
        xs_ref[r - 1] = xe_ref[r:r + n_shift]
    first = CONV_HIST - (CONV_W - 1)

    def row_chunk(rc, carry):
        r0 = pl.multiple_of(rc * rows, rows)
        for cb in range(D_MODEL // LANES):
            cs = slice(cb * LANES, (cb + 1) * LANES)
            acc = jnp.broadcast_to(b_ref[:, cs], (rows, LANES))
            for tap in range(CONV_W):
                shift = (first + tap) % SUBLANES
                base = first + tap - shift
                if shift == 0:
                    win = xe_ref[pl.ds(r0 + base, rows), cs]
                else:
                    win = xs_ref[shift - 1, pl.ds(r0 + base, rows), cs]
                acc = acc + w_ref[tap:tap + 1, cs] * win
            y_ref[pl.ds(r0, rows), cs] = acc
        return carry

    lax.fori_loop(0, tm // rows, row_chunk, 0)

    y = y_ref[...]
    mu = jnp.mean(y, axis=-1, keepdims=True)
    yc = y - mu
    var = jnp.mean(yc * yc, axis=-1, keepdims=True)
    z = yc * lax.rsqrt(var + EPS) * lg_ref[...] + lb_ref[...]
    c_ref[...] = _dot(jax.nn.silu(z).astype(BF16), wo_ref[...])


def _conv_module(u, hist, conv_dw, conv_b, ln_g, ln_b, w_conv_out, n_seq, tm, rows=16):
    t, d = u.shape
    nt = t // (n_seq * tm)
    kern = functools.partial(_conv_kernel, tm=tm, rows=rows)
    row = lambda b, j: (b * nt + j, 0)
    return pl.pallas_call(
        kern,
        grid=(n_seq, nt),
        in_specs=[pl.BlockSpec((tm, d), row), pl.BlockSpec((1, CONV_HIST, d), lambda b, j: (b, 0, 0)),
                  _const_spec((CONV_HIST, d)), _const_spec((1, d)), _const_spec((1, d)), _const_spec((1, d)),
                  _const_spec((d, d))],
        out_specs=pl.BlockSpec((tm, d), row),
        out_shape=jax.ShapeDtypeStruct((t, d), F32),
        scratch_shapes=[pltpu.VMEM((CONV_HIST + tm, d), F32),
                        pltpu.VMEM((SUBLANES - 1, CONV_HIST + tm - SUBLANES, d), F32), pltpu.VMEM((tm, d), F32)],
        compiler_params=_params(2),
        name="conv_module",
    )(u, hist, conv_dw, conv_b, ln_g, ln_b, w_conv_out)


def _attend_pair(q_blk, k_win, v_win, bias, row_mask):
    lane = lax.broadcasted_iota(jnp.int32, q_blk.shape, 1)
    zero = jnp.zeros_like(q_blk)
    qq = jnp.concatenate([jnp.where(lane < HEAD_DIM, q_blk, zero), jnp.where(lane >= HEAD_DIM, q_blk, zero)], axis=0)
    s = lax.dot_general(qq, k_win, (((1,), (1,)), ((), ())), preferred_element_type=F32)
    s = s + bias
    if row_mask is not None:
        s = s + row_mask
    e = jnp.exp(s - jnp.max(s, axis=-1, keepdims=True))
    o = _dot(e.astype(BF16), v_win) / jnp.sum(e, axis=-1, keepdims=True)
    qb = q_blk.shape[0]
    lane_o = lax.broadcasted_iota(jnp.int32, (qb, LANES), 1)
    return jnp.where(lane_o < HEAD_DIM, o[:qb], o[qb:])


def _attn_prompt_kernel(q_ref, kp_ref, kc_ref, vp_ref, vc_ref, bm_ref, wo_ref, a_ref, kcat_ref, vcat_ref, att_ref, *, tm):
    j = pl.program_id(1)
    kcat_ref[0:tm] = kp_ref[...]
    kcat_ref[tm:2 * tm] = kc_ref[...]
    vcat_ref[0:tm] = vp_ref[...]
    vcat_ref[tm:2 * tm] = vc_ref[...]
    key_idx = lax.broadcasted_iota(jnp.int32, (1, K_WINDOW), 1)

    def q_block(qb, carry):
        r0 = pl.multiple_of(qb * Q_BLOCK, Q_BLOCK)
        row_mask = jnp.where((j == 0) & (key_idx + r0 < tm), NEG_INF, 0.0).astype(F32)
        for hp in range(N_PAIRS):
            cs = slice(hp * LANES, (hp + 1) * LANES)
            att_ref[pl.ds(r0, Q_BLOCK), cs] = _attend_pair(
                q_ref[pl.ds(r0, Q_BLOCK), cs], kcat_ref[pl.ds(r0, K_WINDOW), cs], vcat_ref[pl.ds(r0, K_WINDOW), cs],
                bm_ref[hp], row_mask)
        return carry

    lax.fori_loop(0, tm // Q_BLOCK, q_block, 0)
    a_ref[...] = _dot(att_ref[...].astype(BF16), wo_ref[...])


def _attn_prompt(q, k, v, bias, w_att_out, n_seq, tm=ATT_PAST):
    t, d = q.shape
    nt = t // (n_seq * tm)
    cur = lambda b, j: (b * nt + j, 0)
    prev = lambda b, j: (b * nt + jnp.maximum(j - 1, 0), 0)
    return pl.pallas_call(
        functools.partial(_attn_prompt_kernel, tm=tm),
        grid=(n_seq, nt),
        in_specs=[pl.BlockSpec((tm, d), cur), pl.BlockSpec((tm, d), prev), pl.BlockSpec((tm, d), cur),
                  pl.BlockSpec((tm, d), prev), pl.BlockSpec((tm, d), cur),
                  _const_spec(bias.shape), _const_spec((d, d))],
        out_specs=pl.BlockSpec((tm, d), cur),
        out_shape=jax.ShapeDtypeStruct((t, d), F32),
        scratch_shapes=[pltpu.VMEM((2 * tm, d), BF16), pltpu.VMEM((2 * tm, d), BF16), pltpu.VMEM((tm, d), F32)],
        compiler_params=_params(2),
        name="attn_prompt",
    )(q, k, k, v, v, bias, w_att_out)


def _attn_sample_kernel(q_ref, kc_ref, kn_ref, vc_ref, vn_ref, bm_ref, wo_ref, a_ref, kcat_ref, vcat_ref, att_ref):
    l, t = ATT_PAST, CHUNK
    kcat_ref[0:l] = kc_ref[0].astype(BF16)
    kcat_ref[l:l + t] = kn_ref[...]
    kcat_ref[l + t:K_WINDOW] = jnp.zeros((K_WINDOW - l - t, D_MODEL), BF16)
    vcat_ref[0:l] = vc_ref[0].astype(BF16)
    vcat_ref[l:l + t] = vn_ref[...]
    vcat_ref[l + t:K_WINDOW] = jnp.zeros((K_WINDOW - l - t, D_MODEL), BF16)
    for hp in range(N_PAIRS):
        cs = slice(hp * LANES, (hp + 1) * LANES)
        att_ref[:, cs] = _attend_pair(q_ref[:, cs], kcat_ref[:, cs], vcat_ref[:, cs], bm_ref[hp], None)
    a_ref[...] = _dot(att_ref[...].astype(BF16), wo_ref[...])


def _attn_sample(q, cache_k, k_new, cache_v, v_new, bias, w_att_out):
    t, d = q.shape
    n_seq = t // CHUNK
    row = lambda b: (b, 0)
    cache = lambda b: (b, 0, 0)
    return pl.pallas_call(
        _attn_sample_kernel,
        grid=(n_seq,),
        in_specs=[pl.BlockSpec((CHUNK, d), row), pl.BlockSpec((1, ATT_PAST, d), cache), pl.BlockSpec((CHUNK, d), row),
                  pl.BlockSpec((1, ATT_PAST, d), cache), pl.BlockSpec((CHUNK, d), row),
                  _const_spec(bias.shape), _const_spec((d, d))],
        out_specs=pl.BlockSpec((CHUNK, d), row),
        out_shape=jax.ShapeDtypeStruct((t, d), F32),
        scratch_shapes=[pltpu.VMEM((K_WINDOW, d), BF16), pltpu.VMEM((K_WINDOW, d), BF16), pltpu.VMEM((CHUNK, d), F32)],
        compiler_params=_params(1),
        name="attn_sample",
    )(q, cache_k, k_new, cache_v, v_new, bias, w_att_out)


def _attention_bias(rel_table):
    r = jnp.arange(Q_BLOCK)[:, None]
    c = jnp.arange(K_WINDOW)[None, :]
    dist = r + ATT_PAST - c
    band = jnp.where(r < CHUNK, c < ATT_PAST + CHUNK, c >= CHUNK)
    b = rel_table[:, jnp.clip(dist, -REL_CLIP, REL_CLIP) + REL_CLIP].astype(F32)
    b = jnp.where(band[None], b, NEG_INF)
    prompt = b.reshape(N_PAIRS, 2 * Q_BLOCK, K_WINDOW)
    sample = b[:, :CHUNK].reshape(N_PAIRS, 2 * CHUNK, K_WINDOW)
    return prompt, sample


def _ffn_kernel(x_ref, g_ref, c_ref, a_ref, p_ref, fh_ref, wout_ref, nf_ref, wup_ref, fw_ref, fb_ref, wdn_ref,
                npl_ref, wpg_ref, wpp_ref, nfin_ref, y_ref, fst_ref, up_ref, *, tm, tiles_per_seq, final):
    d = D_MODEL
    i = pl.program_id(0)
    merged = g_ref[:, 0:d] * c_ref[...] + g_ref[:, d:2 * d] * a_ref[...]
    x = x_ref[...] + _dot(merged.astype(BF16), wout_ref[...])

    h = _rms(x, nf_ref[...]).astype(BF16)

    @pl.when(i % tiles_per_seq == 0)
    def _():
        up_ref[0:FFN_HIST] = fh_ref[0]

    @pl.when(i % tiles_per_seq != 0)
    def _():
        up_ref[0:FFN_HIST] = up_ref[tm:tm + FFN_HIST]

    up_ref[FFN_HIST:FFN_HIST + tm] = _dot(h, wup_ref[:, 0:D_FF])
    gate = _dot(h, wup_ref[:, D_FF:2 * D_FF])
    fst_ref[0] = up_ref[tm:tm + FFN_HIST]
    cv = fb_ref[...]
    for tap in range(FFN_W):
        first = FFN_HIST - (FFN_W - 1) + tap
        cv = cv + fw_ref[tap:tap + 1] * up_ref[first:first + tm]
    act = cv * (lax.erf(cv / jnp.sqrt(F32(2.0))) + 1.0) / 2.0 * gate
    x = x + _dot(act.astype(BF16), wdn_ref[...])

    h = _rms(x, npl_ref[...]).astype(BF16)
    x = x + jax.nn.sigmoid(_dot(h, wpg_ref[...])) * _dot(p_ref[...].astype(BF16), wpp_ref[...])
    y_ref[...] = _rms(x, nfin_ref[...]) if final else x


def _ffn(x, g, c, a, p, ffn_hist, w_out, norm_ffn, w_up, ffn_dw, ffn_b, w_down, norm_ple, w_pg, w_pp, norm_final,
         seg, tm, final):
    t, d = x.shape
    n_tiles = t // tm
    tiles_per_seq = seg // tm
    row = lambda i: (i, 0)
    seq = lambda i: (i // tiles_per_seq, 0, 0)
    kern = functools.partial(_ffn_kernel, tm=tm, tiles_per_seq=tiles_per_seq, final=final)
    return pl.pallas_call(
        kern,
        grid=(n_tiles,),
        in_specs=[pl.BlockSpec((tm, d), row), pl.BlockSpec((tm, 2 * d), row), pl.BlockSpec((tm, d), row),
                  pl.BlockSpec((tm, d), row), pl.BlockSpec((tm, PLE_DIM), row),
                  pl.BlockSpec((1, FFN_HIST, D_FF), seq),
                  _const_spec((d, d)), _const_spec((1, d)), _const_spec((d, 2 * D_FF)), _const_spec((FFN_HIST, D_FF)),
                  _const_spec((1, D_FF)), _const_spec((D_FF, d)), _const_spec((1, d)), _const_spec((d, d)),
                  _const_spec((PLE_DIM, d)), _const_spec((1, d))],
        out_specs=[pl.BlockSpec((tm, d), row), pl.BlockSpec((1, FFN_HIST, D_FF), seq)],
        out_shape=[jax.ShapeDtypeStruct((t, d), F32),
                   jax.ShapeDtypeStruct((n_tiles // tiles_per_seq, FFN_HIST, D_FF), F32)],
        scratch_shapes=[pltpu.VMEM((FFN_HIST + tm, D_FF), F32)],
        compiler_params=_params(1),
        name="ffn",
    )(x, g, c, a, p, ffn_hist, w_out, norm_ffn, w_up, ffn_dw, ffn_b, w_down, norm_ple, w_pg, w_pp, norm_final)


def _pad_rows_front(h, rows):
    return jnp.pad(h, ((0, 0), (rows - h.shape[1], 0), (0, 0)))


def kernel(x_prompt, x_sample, p_prompt, p_sample, cache_att_k, cache_att_v, state_conv, state_ffn_conv, norm_mix, w_in, conv_dw, conv_dw_b, conv_ln_g, conv_ln_b, w_conv_out, rel_table, w_att_out, b_gate, w_out, norm_ffn, w_ffn_up, ffn_dw, ffn_dw_b, w_ffn_down, norm_ple, w_ple_gate, w_ple_proj, norm_final):
    depth = w_in.shape[0]
    bp, sp, d = x_prompt.shape
    bs, ss, _ = x_sample.shape
    l_cache = cache_att_k.shape[2]
    xp = x_prompt.reshape(bp * sp, d)
    xs = x_sample.reshape(bs * ss, d)
    row = lambda v: v.reshape(1, -1)

    w_in_b, w_co_b, w_ao_b, w_out_b = (w.astype(BF16) for w in (w_in, w_conv_out, w_att_out, w_out))
    w_up_b, w_dn_b, w_pg_b, w_pp_b = (w.astype(BF16) for w in (w_ffn_up, w_ffn_down, w_ple_gate, w_ple_proj))
    zero_conv = jnp.zeros((bp, CONV_HIST, d), F32)
    zero_ffn = jnp.zeros((bp, FFN_HIST, D_FF), F32)
    nfin = row(norm_final)

    outs = [[] for _ in range(8)]
    for i in range(depth):
        final = i == depth - 1
        bias_p, bias_s = _attention_bias(rel_table[i])
        cdw = jnp.pad(conv_dw[i], ((0, CONV_HIST - CONV_W), (0, 0)))
        fdw = jnp.pad(ffn_dw[i], ((0, FFN_HIST - FFN_W), (0, 0)))
        conv_args = (cdw, row(conv_dw_b[i]), row(conv_ln_g[i]), row(conv_ln_b[i]), w_co_b[i])
        ffn_args = (w_out_b[i], row(norm_ffn[i]), w_up_b[i], fdw, row(ffn_dw_b[i]), w_dn_b[i], row(norm_ple[i]),
                    w_pg_b[i], w_pp_b[i], nfin)

        u, q, k, v, g, kst, vst = _in_proj(xp, row(norm_mix[i]), w_in_b[i], row(b_gate[i]), tiles_per_seq=sp // 512)
        c = _conv_module(u, zero_conv, *conv_args, n_seq=bp, tm=512)
        a = _attn_prompt(q, k, v, bias_p, w_ao_b[i], n_seq=bp)
        xp, fst = _ffn(xp, g, c, a, p_prompt[i].reshape(bp * sp, PLE_DIM), zero_ffn, *ffn_args, seg=sp, tm=256, final=final)
        keep = min(ATT_PAST, sp)
        outs[0].append(kst[:, 512 - keep:].reshape(bp, keep, N_HEADS, HEAD_DIM))
        outs[1].append(vst[:, 512 - keep:].reshape(bp, keep, N_HEADS, HEAD_DIM))
        outs[2].append(u.reshape(bp, sp, d)[:, sp - (CONV_W - 1):])
        outs[3].append(fst[:, FFN_HIST - (FFN_W - 1):])

        ck = cache_att_k[i].reshape(bs, l_cache, d)
        cv = cache_att_v[i].reshape(bs, l_cache, d)
        u, q, k, v, g, kst, vst = _in_proj(xs, row(norm_mix[i]), w_in_b[i], row(b_gate[i]), tiles_per_seq=1)
        c = _conv_module(u, _pad_rows_front(state_conv[i], CONV_HIST), *conv_args, n_seq=bs, tm=ss)
        a = _attn_sample(q, ck, k, cv, v, bias_s, w_ao_b[i])
        xs, fst = _ffn(xs, g, c, a, p_sample[i].reshape(bs * ss, PLE_DIM), _pad_rows_front(state_ffn_conv[i], FFN_HIST),
                       *ffn_args, seg=ss, tm=ss, final=final)
        k_new = kst.reshape(bs, ss, d)
        v_new = vst.reshape(bs, ss, d)
        outs[4].append(jnp.concatenate([ck, k_new], axis=1)[:, ss:].reshape(bs, l_cache, N_HEADS, HEAD_DIM))
        outs[5].append(jnp.concatenate([cv, v_new], axis=1)[:, ss:].reshape(bs, l_cache, N_HEADS, HEAD_DIM))
        outs[6].append(jnp.concatenate([state_conv[i], u.reshape(bs, ss, d)], axis=1)[:, ss:])
        outs[7].append(fst[:, FFN_HIST - (FFN_W - 1):])

    stacked = [jnp.stack(o) for o in outs]
    return (xp.reshape(bp, sp, d), xs.reshape(bs, ss, d), *stacked)
```

```python
import functools
import math

import jax
import jax.numpy as jnp
from jax import lax
from jax.experimental import pallas as pl
from jax.experimental.pallas import tpu as pltpu

D_MODEL = 1024
N_HEADS = 16
HEAD_DIM = 64
CHUNK = 64
N_PAST_CHUNKS = 8
ATT_PAST = N_PAST_CHUNKS * CHUNK
REL_CLIP = 128
CONV_W = 31
D_FF = 2816
FFN_W = 3
PLE_DIM = 256
EPS = 1e-6
NEG_INF = -1e30

LANES = 128
SUBLANES = 8
N_PAIRS = D_MODEL // LANES
Q_BLOCK = 2 * CHUNK
K_WINDOW = ATT_PAST + Q_BLOCK
PIPE = 4
CONV_HIST = 32
FFN_HIST = 8
TABLE_PAD = 384
VMEM_LIMIT = 60 * 1024 * 1024
LOG2E = math.log2(math.e)
Q_SCALE = HEAD_DIM ** -0.5 * LOG2E

BF16 = jnp.bfloat16
F32 = jnp.float32


def _const_spec(shape):
    zeros = (0,) * len(shape)
    return pl.BlockSpec(shape, lambda *_: zeros, pipeline_mode=pl.Buffered(1))


def _layer_spec(shape, layer):
    zeros = (0,) * len(shape)
    return pl.BlockSpec((None, *shape), lambda *_: (layer, *zeros), pipeline_mode=pl.Buffered(1))


def _params(n_axes):
    return pltpu.CompilerParams(dimension_semantics=("arbitrary",) * n_axes, vmem_limit_bytes=VMEM_LIMIT)


def _rms(x, w):
    return x * lax.rsqrt(jnp.mean(x * x, axis=-1, keepdims=True) + EPS) * w


def _dot(a, b):
    return jnp.dot(a, b, preferred_element_type=F32)


def _in_proj_kernel(x_ref, nw_ref, w_ref, bg_ref, u_ref, q_ref, k_ref, v_ref, g_ref, kst_ref, vst_ref):
    d = D_MODEL
    h = _rms(x_ref[...], nw_ref[...]).astype(BF16)
    glu_a = _dot(h, w_ref[:, 0:d])
    glu_b = _dot(h, w_ref[:, d:2 * d])
    u_ref[...] = glu_a * jax.nn.sigmoid(glu_b)

    def put_pairs(dst_ref, val):
        for hp in range(N_PAIRS):
            dst_ref[hp] = val[:, hp * LANES:(hp + 1) * LANES]

    put_pairs(q_ref, (_dot(h, w_ref[:, 2 * d:3 * d]) * Q_SCALE).astype(BF16))
    k = _dot(h, w_ref[:, 3 * d:4 * d])
    put_pairs(k_ref, k.astype(BF16))
    kst_ref[0] = k
    v = _dot(h, w_ref[:, 4 * d:5 * d])
    put_pairs(v_ref, v.astype(BF16))
    vst_ref[0] = v
    g_ref[...] = jax.nn.sigmoid(_dot(h, w_ref[:, 5 * d:7 * d]) + bg_ref[...])


def _in_proj(x, norm_w, w_in, b_gate, layer, tiles_per_seq, tm=512):
    t, d = x.shape
    n_tiles = t // tm
    n_seq = n_tiles // tiles_per_seq
    row = lambda i: (i, 0)
    st = lambda i: (i // tiles_per_seq, 0, 0)
    pairs = pl.BlockSpec((N_PAIRS, tm, LANES), lambda i: (0, i, 0))
    pairs_shape = jax.ShapeDtypeStruct((N_PAIRS, t, LANES), BF16)
    return pl.pallas_call(
        _in_proj_kernel,
        grid=(n_tiles,),
        in_specs=[pl.BlockSpec((tm, d), row), _layer_spec((1, d), layer), _layer_spec((d, 7 * d), layer),
                  _layer_spec((1, 2 * d), layer)],
        out_specs=[pl.BlockSpec((tm, d), row), pairs, pairs, pairs, pl.BlockSpec((tm, 2 * d), row),
                   pl.BlockSpec((1, tm, d), st), pl.BlockSpec((1, tm, d), st)],
        out_shape=[jax.ShapeDtypeStruct((t, d), F32), pairs_shape, pairs_shape, pairs_shape,
                   jax.ShapeDtypeStruct((t, 2 * d), F32),
                   jax.ShapeDtypeStruct((n_seq, tm, d), F32), jax.ShapeDtypeStruct((n_seq, tm, d), F32)],
        compiler_params=_params(1),
        name="in_proj",
    )(x, norm_w, w_in, b_gate)


def _conv_kernel(u_ref, hist_ref, w_ref, b_ref, lg_ref, lb_ref, wo_ref, c_ref, xe_ref, xs_ref, y_ref, *, tm, rows):
    j = pl.program_id(1)

    @pl.when(j == 0)
    def _():
        for cb in range(D_MODEL // LANES):
            xe_ref[cb, 0:CONV_HIST] = hist_ref[:, cb * LANES:(cb + 1) * LANES]

    @pl.when(j > 0)
    def _():
        for cb in range(D_MODEL // LANES):
            xe_ref[cb, 0:CONV_HIST] = xe_ref[cb, tm:tm + CONV_HIST]

    n_shift = CONV_HIST + tm - SUBLANES
    first = CONV_HIST - (CONV_W - 1)
    for cb in range(D_MODEL // LANES):
        cs = slice(cb * LANES, (cb + 1) * LANES)
        xe_ref[cb, CONV_HIST:CONV_HIST + tm] = u_ref[:, cs]
        for r in range(1, SUBLANES):
            xs_ref[cb, r - 1] = xe_ref[cb, r:r + n_shift]

        def row_chunk(rc, carry, cb=cb, cs=cs):
            r0 = pl.multiple_of(rc * rows, rows)
            acc = jnp.broadcast_to(b_ref[:, cs], (rows, LANES))
            for tap in range(CONV_W):
                shift = (first + tap) % SUBLANES
                base = first + tap - shift
                if shift == 0:
                    win = xe_ref[cb, pl.ds(r0 + base, rows), :]
                else:
                    win = xs_ref[cb, shift - 1, pl.ds(r0 + base, rows), :]
                acc = acc + w_ref[tap:tap + 1, cs] * win
            y_ref[pl.ds(r0, rows), cs] = acc
            return carry

        lax.fori_loop(0, tm // rows, row_chunk, 0)

    y = y_ref[...]
    mu = jnp.mean(y, axis=-1, keepdims=True)
    yc = y - mu
    var = jnp.mean(yc * yc, axis=-1, keepdims=True)
    z = yc * lax.rsqrt(var + EPS) * lg_ref[...] + lb_ref[...]
    c_ref[...] = _dot(jax.nn.silu(z).astype(BF16), wo_ref[...])


def _conv_module(u, hist, hist_layer, conv_dw, conv_b, ln_g, ln_b, w_conv_out, layer, n_seq, tm, rows=64):
    t, d = u.shape
    nt = t // (n_seq * tm)
    per_seq_hist = hist.shape[1] > 1
    kern = functools.partial(_conv_kernel, tm=tm, rows=rows)
    row = lambda b, j: (b * nt + j, 0)
    return pl.pallas_call(
        kern,
        grid=(n_seq, nt),
        in_specs=[pl.BlockSpec((tm, d), row),
                  pl.BlockSpec((None, None, CONV_HIST, d), lambda b, j: (hist_layer, b if per_seq_hist else 0, 0, 0)),
                  _layer_spec((CONV_W, d), layer), _layer_spec((1, d), layer), _layer_spec((1, d), layer),
                  _layer_spec((1, d), layer), _layer_spec((d, d), layer)],
        out_specs=pl.BlockSpec((tm, d), row),
        out_shape=jax.ShapeDtypeStruct((t, d), F32),
        scratch_shapes=[pltpu.VMEM((d // LANES, CONV_HIST + tm, LANES), F32),
                        pltpu.VMEM((d // LANES, SUBLANES - 1, CONV_HIST + tm - SUBLANES, LANES), F32),
                        pltpu.VMEM((tm, d), F32)],
        compiler_params=_params(2),
        name="conv_module",
    )(u, hist, conv_dw, conv_b, ln_g, ln_b, w_conv_out)


def _bias_kernel(t_ref, bp_ref, bs_ref):
    t = t_ref[...]
    hi = t.astype(BF16)
    rest = t - hi.astype(F32)
    mid = rest.astype(BF16)
    lo = (rest - mid.astype(F32)).astype(BF16)
    idx = lax.broadcasted_iota(jnp.int32, (TABLE_PAD, K_WINDOW), 0)
    col = lax.broadcasted_iota(jnp.int32, (TABLE_PAD, K_WINDOW), 1)
    sel = jnp.where(col < ATT_PAST + CHUNK, jnp.clip(ATT_PAST - col, -REL_CLIP, REL_CLIP) + REL_CLIP, 2 * REL_CLIP)
    onehot = jnp.where(idx == sel, 1.0, 0.0).astype(BF16)
    row0 = (_dot(hi, onehot) + _dot(mid, onehot) + _dot(lo, onehot)) * LOG2E
    r = lax.broadcasted_iota(jnp.int32, (Q_BLOCK, K_WINDOW), 0)
    c = lax.broadcasted_iota(jnp.int32, (Q_BLOCK, K_WINDOW), 1)
    band = ((r < CHUNK) & (c < ATT_PAST + CHUNK)) | ((r >= CHUNK) & (c >= CHUNK))
    for h in range(N_HEADS):
        x = pltpu.roll(jnp.broadcast_to(row0[h:h + 1], (Q_BLOCK, K_WINDOW)), 0, 1, stride=1, stride_axis=0)
        x = jnp.where(band, x, NEG_INF)
        bp_ref[h] = x
        bs_ref[h] = x[:CHUNK]


def _attention_bias(rel_table):
    assert ATT_PAST >= REL_CLIP and rel_table.shape[2] == 2 * REL_CLIP + 1
    depth = rel_table.shape[0]
    table = jnp.pad(rel_table, ((0, 0), (0, 0), (0, TABLE_PAD - rel_table.shape[2])))
    bp, bs = pl.pallas_call(
        _bias_kernel,
        grid=(depth,),
        in_specs=[pl.BlockSpec((None, N_HEADS, TABLE_PAD), lambda l: (l, 0, 0))],
        out_specs=[pl.BlockSpec((None, N_HEADS, Q_BLOCK, K_WINDOW), lambda l: (l, 0, 0, 0)),
                   pl.BlockSpec((None, N_HEADS, CHUNK, K_WINDOW), lambda l: (l, 0, 0, 0))],
        out_shape=[jax.ShapeDtypeStruct((depth, N_HEADS, Q_BLOCK, K_WINDOW), F32),
                   jax.ShapeDtypeStruct((depth, N_HEADS, CHUNK, K_WINDOW), F32)],
        compiler_params=_params(1),
        name="rel_bias",
    )(table)
    return bp.reshape(depth, N_PAIRS, 2 * Q_BLOCK, K_WINDOW), bs.reshape(depth, N_PAIRS, 2 * CHUNK, K_WINDOW)


def _scores(q_blk, k_win):
    lane = lax.broadcasted_iota(jnp.int32, q_blk.shape, 1)
    zero = jnp.zeros_like(q_blk)
    qq = jnp.concatenate([jnp.where(lane < HEAD_DIM, q_blk, zero), jnp.where(lane >= HEAD_DIM, q_blk, zero)], axis=0)
    return lax.dot_general(qq, k_win, (((1,), (1,)), ((), ())), preferred_element_type=F32)


def _softmax_numerator(s):
    return jnp.exp2(s - jnp.max(s, axis=-1, keepdims=True)).astype(BF16)


def _weighted_values(e, v_win):
    qb = e.shape[0] // 2
    o = _dot(e, jnp.concatenate([v_win, jnp.ones_like(v_win)], axis=1))
    o = o[:, :LANES] / o[:, LANES:]
    lane = lax.broadcasted_iota(jnp.int32, (qb, LANES), 1)
    return jnp.where(lane < HEAD_DIM, o[:qb], o[qb:])


def _attn_prompt_kernel(q_ref, kp_ref, kc_ref, vp_ref, vc_ref, bm_ref, wo_ref, a_ref,
                        kcat_ref, vcat_ref, s_ref, e_ref, att_ref, *, tm):
    j = pl.program_id(1)
    n_qb = tm // Q_BLOCK
    n_steps = N_PAIRS * n_qb
    for hp in range(N_PAIRS):
        kcat_ref[hp, 0:tm] = kp_ref[hp]
        kcat_ref[hp, tm:2 * tm] = kc_ref[hp]
        vcat_ref[hp, 0:tm] = vp_ref[hp]
        vcat_ref[hp, tm:2 * tm] = vc_ref[hp]
    key_idx = lax.broadcasted_iota(jnp.int32, (1, K_WINDOW), 1)

    def where(n):
        r0 = (n % n_qb) * Q_BLOCK
        return n // n_qb, (r0 if isinstance(n, int) else pl.multiple_of(r0, Q_BLOCK))

    def scores(n):
        hp, r0 = where(n)
        return _scores(q_ref[hp, pl.ds(r0, Q_BLOCK), :], kcat_ref[hp, pl.ds(r0, K_WINDOW), :])

    def softmax(n, s):
        hp, r0 = where(n)
        row_mask = jnp.where((j == 0) & (key_idx + r0 < tm), NEG_INF, 0.0).astype(F32)
        return _softmax_numerator(s + bm_ref[hp] + row_mask)

    def values(n, e):
        hp, r0 = where(n)
        att_ref[hp, pl.ds(r0, Q_BLOCK), :] = _weighted_values(e, vcat_ref[hp, pl.ds(r0, K_WINDOW), :])

    def step(n, slot):
        if isinstance(n, int):
            first, mid, last = 0 <= n + lag < n_steps, 0 <= n < n_steps, 0 <= n - lag < n_steps
        else:
            first = mid = last = True
        if first:
            s_ref[(slot + lag) % PIPE] = scores(n + lag)
        if mid:
            e_ref[slot] = softmax(n, s_ref[slot])
        if last:
            values(n - lag, e_ref[(slot - lag) % PIPE])

    lag = PIPE // 2
    for n in range(-lag, lag):
        step(n, n % PIPE)

    def body(m, carry):
        for i in range(PIPE):
            step(lag + m * PIPE + i, (lag + i) % PIPE)
        return carry

    lax.fori_loop(0, (n_steps - 2 * lag) // PIPE, body, 0)
    for n in range(n_steps - lag, n_steps + lag):
        step(n, n % PIPE)
    att = jnp.concatenate([att_ref[hp].astype(BF16) for hp in range(N_PAIRS)], axis=1)
    a_ref[...] = _dot(att, wo_ref[...])


def _attn_prompt(q, k, v, bias, w_att_out, layer, n_seq, tm=ATT_PAST):
    _, t, _ = q.shape
    d = D_MODEL
    nt = t // (n_seq * tm)
    assert (N_PAIRS * (tm // Q_BLOCK) - PIPE) % PIPE == 0
    cur = lambda b, j: (0, b * nt + j, 0)
    prev = lambda b, j: (0, b * nt + jnp.maximum(j - 1, 0), 0)
    blk = (N_PAIRS, tm, LANES)
    return pl.pallas_call(
        functools.partial(_attn_prompt_kernel, tm=tm),
        grid=(n_seq, nt),
        in_specs=[pl.BlockSpec(blk, cur), pl.BlockSpec(blk, prev), pl.BlockSpec(blk, cur),
                  pl.BlockSpec(blk, prev), pl.BlockSpec(blk, cur),
                  _layer_spec(bias.shape[1:], layer), _layer_spec((d, d), layer)],
        out_specs=pl.BlockSpec((tm, d), lambda b, j: (b * nt + j, 0)),
        out_shape=jax.ShapeDtypeStruct((t, d), F32),
        scratch_shapes=[pltpu.VMEM((N_PAIRS, 2 * tm, LANES), BF16), pltpu.VMEM((N_PAIRS, 2 * tm, LANES), BF16),
                        pltpu.VMEM((PIPE, 2 * Q_BLOCK, K_WINDOW), F32), pltpu.VMEM((PIPE, 2 * Q_BLOCK, K_WINDOW), BF16),
                        pltpu.VMEM((N_PAIRS, tm, LANES), F32)],
        compiler_params=_params(2),
        name="attn_prompt",
    )(q, k, k, v, v, bias, w_att_out)


def _attn_sample_kernel(q_ref, kc_ref, kn_ref, knf_ref, vc_ref, vn_ref, vnf_ref, bm_ref, wo_ref,
                        a_ref, ko_ref, vo_ref, kcat_ref, vcat_ref):
    l, t = ATT_PAST, CHUNK
    pad = jnp.zeros((K_WINDOW - l - t, LANES), BF16)
    for cache_ref, new_ref, new_f32_ref, out_ref, cat_ref in ((kc_ref, kn_ref, knf_ref, ko_ref, kcat_ref),
                                                              (vc_ref, vn_ref, vnf_ref, vo_ref, vcat_ref)):
        for hp in range(N_PAIRS):
            cat_ref[hp, 0:l] = cache_ref[:, hp * LANES:(hp + 1) * LANES].astype(BF16)
            cat_ref[hp, l:l + t] = new_ref[hp]
            cat_ref[hp, l + t:K_WINDOW] = pad
        out_ref[0:l - t] = cache_ref[t:l]
        out_ref[l - t:l] = new_f32_ref[...]
    att = []
    for hp in range(N_PAIRS):
        e = _softmax_numerator(_scores(q_ref[hp], kcat_ref[hp]) + bm_ref[hp])
        att.append(_weighted_values(e, vcat_ref[hp]).astype(BF16))
    a_ref[...] = _dot(jnp.concatenate(att, axis=1), wo_ref[...])


def _attn_sample(q, cache_k, k_new, k_new_f32, cache_v, v_new, v_new_f32, bias, w_att_out, layer):
    _, t, _ = q.shape
    d = D_MODEL
    n_seq = t // CHUNK
    row = lambda b: (b, 0)
    pairs = pl.BlockSpec((N_PAIRS, CHUNK, LANES), lambda b: (0, b, 0))
    cache = pl.BlockSpec((None, None, ATT_PAST, d), lambda b: (layer, b, 0, 0))
    new_cache = pl.BlockSpec((None, ATT_PAST, d), lambda b: (b, 0, 0))
    return pl.pallas_call(
        _attn_sample_kernel,
        grid=(n_seq,),
        in_specs=[pairs, cache, pairs, pl.BlockSpec((CHUNK, d), row), cache, pairs, pl.BlockSpec((CHUNK, d), row),
                  _layer_spec(bias.shape[1:], layer), _layer_spec((d, d), layer)],
        out_specs=[pl.BlockSpec((CHUNK, d), row), new_cache, new_cache],
        out_shape=[jax.ShapeDtypeStruct((t, d), F32), jax.ShapeDtypeStruct((n_seq, ATT_PAST, d), F32),
                   jax.ShapeDtypeStruct((n_seq, ATT_PAST, d), F32)],
        scratch_shapes=[pltpu.VMEM((N_PAIRS, K_WINDOW, LANES), BF16), pltpu.VMEM((N_PAIRS, K_WINDOW, LANES), BF16)],
        compiler_params=_params(1),
        name="attn_sample",
    )(q, cache_k, k_new, k_new_f32, cache_v, v_new, v_new_f32, bias, w_att_out)


def _ffn_kernel(x_ref, g_ref, c_ref, a_ref, p_ref, fh_ref, wout_ref, nf_ref, wup_ref, fw_ref, fb_ref, wdn_ref,
                npl_ref, wpg_ref, wpp_ref, nfin_ref, y_ref, fst_ref, up_ref, *, tm, tiles_per_seq, final):
    d = D_MODEL
    i = pl.program_id(0)
    merged = g_ref[:, 0:d] * c_ref[...] + g_ref[:, d:2 * d] * a_ref[...]
    x = x_ref[...] + _dot(merged.astype(BF16), wout_ref[...])

    h = _rms(x, nf_ref[...]).astype(BF16)

    @pl.when(i % tiles_per_seq == 0)
    def _():
        up_ref[0:FFN_HIST] = fh_ref[...]

    @pl.when(i % tiles_per_seq != 0)
    def _():
        up_ref[0:FFN_HIST] = up_ref[tm:tm + FFN_HIST]

    up_ref[FFN_HIST:FFN_HIST + tm] = _dot(h, wup_ref[:, 0:D_FF])
    gate = _dot(h, wup_ref[:, D_FF:2 * D_FF])
    fst_ref[0] = up_ref[tm:tm + FFN_HIST]
    cv = fb_ref[...]
    for tap in range(FFN_W):
        first = FFN_HIST - (FFN_W - 1) + tap
        cv = cv + fw_ref[tap:tap + 1] * up_ref[first:first + tm]
    act = cv * (lax.erf(cv / jnp.sqrt(F32(2.0))) + 1.0) / 2.0 * gate
    x = x + _dot(act.astype(BF16), wdn_ref[...])

    h = _rms(x, npl_ref[...]).astype(BF16)
    x = x + jax.nn.sigmoid(_dot(h, wpg_ref[...])) * _dot(p_ref[...].astype(BF16), wpp_ref[...])
    y_ref[...] = _rms(x, nfin_ref[...]) if final else x


def _ffn(x, g, c, a, p, ffn_hist, hist_layer, w_out, norm_ffn, w_up, ffn_dw, ffn_b, w_down, norm_ple, w_pg, w_pp,
         norm_final, layer, seg, tm, final):
    t, d = x.shape
    n_tiles = t // tm
    tiles_per_seq = seg // tm
    per_seq_hist = ffn_hist.shape[1] > 1
    row = lambda i: (i, 0)
    kern = functools.partial(_ffn_kernel, tm=tm, tiles_per_seq=tiles_per_seq, final=final)
    return pl.pallas_call(
        kern,
        grid=(n_tiles,),
        in_specs=[pl.BlockSpec((tm, d), row), pl.BlockSpec((tm, 2 * d), row), pl.BlockSpec((tm, d), row),
                  pl.BlockSpec((tm, d), row), pl.BlockSpec((None, tm, PLE_DIM), lambda i: (layer, i, 0)),
                  pl.BlockSpec((None, None, FFN_HIST, D_FF),
                               lambda i: (hist_layer, (i // tiles_per_seq) if per_seq_hist else 0, 0, 0)),
                  _layer_spec((d, d), layer), _layer_spec((1, d), layer), _layer_spec((d, 2 * D_FF), layer),
                  _layer_spec((FFN_W, D_FF), layer), _layer_spec((1, D_FF), layer), _layer_spec((D_FF, d), layer),
                  _layer_spec((1, d), layer), _layer_spec((d, d), layer), _layer_spec((PLE_DIM, d), layer),
                  _const_spec((1, d))],
        out_specs=[pl.BlockSpec((tm, d), row), pl.BlockSpec((1, FFN_HIST, D_FF), lambda i: (i // tiles_per_seq, 0, 0))],
        out_shape=[jax.ShapeDtypeStruct((t, d), F32),
                   jax.ShapeDtypeStruct((n_tiles // tiles_per_seq, FFN_HIST, D_FF), F32)],
        scratch_shapes=[pltpu.VMEM((FFN_HIST + tm, D_FF), F32)],
        compiler_params=_params(1),
        name="ffn",
    )(x, g, c, a, p, ffn_hist, w_out, norm_ffn, w_up, ffn_dw, ffn_b, w_down, norm_ple, w_pg, w_pp, norm_final)


def _pad_rows_front(h, rows):
    return jnp.pad(h, ((0, 0),) * (h.ndim - 2) + ((rows - h.shape[-2], 0), (0, 0)))


def kernel(x_prompt, x_sample, p_prompt, p_sample, cache_att_k, cache_att_v, state_conv, state_ffn_conv, norm_mix, w_in, conv_dw, conv_dw_b, conv_ln_g, conv_ln_b, w_conv_out, rel_table, w_att_out, b_gate, w_out, norm_ffn, w_ffn_up, ffn_dw, ffn_dw_b, w_ffn_down, norm_ple, w_ple_gate, w_ple_proj, norm_final):
    depth = w_in.shape[0]
    bp, sp, d = x_prompt.shape
    bs, ss, _ = x_sample.shape
    l_cache = cache_att_k.shape[2]
    xp = x_prompt.reshape(bp * sp, d)
    xs = x_sample.reshape(bs * ss, d)
    pp = p_prompt.reshape(depth, bp * sp, PLE_DIM)
    ps = p_sample.reshape(depth, bs * ss, PLE_DIM)
    ck = cache_att_k.reshape(depth, bs, l_cache, d)
    cv = cache_att_v.reshape(depth, bs, l_cache, d)
    rows = lambda v: v.reshape(v.shape[0], 1, v.shape[1])

    w_in_b, w_co_b, w_ao_b, w_out_b = (w.astype(BF16) for w in (w_in, w_conv_out, w_att_out, w_out))
    w_up_b, w_dn_b, w_pg_b, w_pp_b = (w.astype(BF16) for w in (w_ffn_up, w_ffn_down, w_ple_gate, w_ple_proj))
    bias_p, bias_s = _attention_bias(rel_table)
    conv_hist_s = _pad_rows_front(state_conv, CONV_HIST)
    ffn_hist_s = _pad_rows_front(state_ffn_conv, FFN_HIST)
    conv_hist_p = jnp.zeros((1, 1, CONV_HIST, d), F32)
    ffn_hist_p = jnp.zeros((1, 1, FFN_HIST, D_FF), F32)
    in_args = (rows(norm_mix), w_in_b, rows(b_gate))
    conv_args = (conv_dw, rows(conv_dw_b), rows(conv_ln_g), rows(conv_ln_b), w_co_b)
    ffn_args = (w_out_b, rows(norm_ffn), w_up_b, ffn_dw, rows(ffn_dw_b), w_dn_b, rows(norm_ple), w_pg_b, w_pp_b,
                norm_final.reshape(1, d))
    keep = min(ATT_PAST, sp)

    outs = [[] for _ in range(8)]
    for i in range(depth):
        final = i == depth - 1
        u, q, k, v, g, kst, vst = _in_proj(xp, *in_args, layer=i, tiles_per_seq=sp // 512)
        c = _conv_module(u, conv_hist_p, 0, *conv_args, layer=i, n_seq=bp, tm=512)
        a = _attn_prompt(q, k, v, bias_p, w_ao_b, layer=i, n_seq=bp)
        xp, fst = _ffn(xp, g, c, a, pp, ffn_hist_p, 0, *ffn_args, layer=i, seg=sp, tm=256, final=final)
        outs[0].append(kst[:, 512 - keep:])
        outs[1].append(vst[:, 512 - keep:])
        outs[2].append(u.reshape(bp, sp, d)[:, sp - (CONV_W - 1):])
        outs[3].append(fst[:, FFN_HIST - (FFN_W - 1):])

        u, q, k, v, g, kst, vst = _in_proj(xs, *in_args, layer=i, tiles_per_seq=1)
        c = _conv_module(u, conv_hist_s, i, *conv_args, layer=i, n_seq=bs, tm=ss)
        a, k_cache, v_cache = _attn_sample(q, ck, k, kst[0], cv, v, vst[0], bias_s, w_ao_b, layer=i)
        xs, fst = _ffn(xs, g, c, a, ps, ffn_hist_s, i, *ffn_args, layer=i, seg=ss, tm=ss, final=final)
        outs[4].append(k_cache)
        outs[5].append(v_cache)
        outs[6].append(jnp.concatenate([state_conv[i], u.reshape(bs, ss, d)], axis=1)[:, ss:])
        outs[7].append(fst[:, FFN_HIST - (FFN_W - 1):])

    heads = lambda o: jnp.stack(o).reshape(depth, -1, o[0].shape[1], N_HEADS, HEAD_DIM)
    return (xp.reshape(bp, sp, d), xs.reshape(bs, ss, d), heads(outs[0]), heads(outs[1]), jnp.stack(outs[2]),
            jnp.stack(outs[3]), heads(outs[4]), heads(outs[5]), jnp.stack(outs[6]), jnp.stack(outs[7]))
```

```python
import functools
import math

import jax
import jax.numpy as jnp
from jax import lax
from jax.experimental import pallas as pl
from jax.experimental.pallas import tpu as pltpu

D_MODEL = 1024
N_HEADS = 16
HEAD_DIM = 64
CHUNK = 64
N_PAST_CHUNKS = 8
ATT_PAST = N_PAST_CHUNKS * CHUNK
REL_CLIP = 128
CONV_W = 31
D_FF = 2816
FFN_W = 3
PLE_DIM = 256
EPS = 1e-6
NEG_INF = -1e30

LANES = 128
SUBLANES = 8
N_PAIRS = D_MODEL // LANES
Q_BLOCK = 2 * CHUNK
K_WINDOW = ATT_PAST + Q_BLOCK
PIPE = 4
CONV_HIST = 32
FFN_HIST = 8
TABLE_PAD = 384
VMEM_LIMIT = 60 * 1024 * 1024
LOG2E = math.log2(math.e)
Q_SCALE = HEAD_DIM ** -0.5 * LOG2E

BF16 = jnp.bfloat16
F32 = jnp.float32


def _const_spec(shape):
    zeros = (0,) * len(shape)
    return pl.BlockSpec(shape, lambda *_: zeros, pipeline_mode=pl.Buffered(1))


def _layer_spec(shape, layer):
    zeros = (0,) * len(shape)
    return pl.BlockSpec((None, *shape), lambda *_: (layer, *zeros), pipeline_mode=pl.Buffered(1))


def _params(n_axes):
    return pltpu.CompilerParams(dimension_semantics=("arbitrary",) * n_axes, vmem_limit_bytes=VMEM_LIMIT)


def _rms(x, w):
    return x * lax.rsqrt(jnp.mean(x * x, axis=-1, keepdims=True) + EPS) * w


def _dot(a, b):
    return jnp.dot(a, b, preferred_element_type=F32)


def _in_proj_kernel(x_ref, nw_ref, w_ref, bg_ref, u_ref, q_ref, k_ref, v_ref, g_ref, kst_ref, vst_ref):
    d = D_MODEL
    h = _rms(x_ref[...], nw_ref[...]).astype(BF16)
    glu_a = _dot(h, w_ref[:, 0:d])
    glu_b = _dot(h, w_ref[:, d:2 * d])
    u_ref[...] = glu_a * jax.nn.sigmoid(glu_b)

    def put_pairs(dst_ref, val):
        for hp in range(N_PAIRS):
            dst_ref[hp] = val[:, hp * LANES:(hp + 1) * LANES]

    put_pairs(q_ref, (_dot(h, w_ref[:, 2 * d:3 * d]) * Q_SCALE).astype(BF16))
    k = _dot(h, w_ref[:, 3 * d:4 * d])
    put_pairs(k_ref, k.astype(BF16))
    kst_ref[0] = k
    v = _dot(h, w_ref[:, 4 * d:5 * d])
    put_pairs(v_ref, v.astype(BF16))
    vst_ref[0] = v
    g_ref[...] = jax.nn.sigmoid(_dot(h, w_ref[:, 5 * d:7 * d]) + bg_ref[...])


def _in_proj(x, norm_w, w_in, b_gate, layer, tiles_per_seq, tm=512):
    t, d = x.shape
    n_tiles = t // tm
    n_seq = n_tiles // tiles_per_seq
    row = lambda i: (i, 0)
    st = lambda i: (i // tiles_per_seq, 0, 0)
    pairs = pl.BlockSpec((N_PAIRS, tm, LANES), lambda i: (0, i, 0))
    pairs_shape = jax.ShapeDtypeStruct((N_PAIRS, t, LANES), BF16)
    return pl.pallas_call(
        _in_proj_kernel,
        grid=(n_tiles,),
        in_specs=[pl.BlockSpec((tm, d), row), _layer_spec((1, d), layer), _layer_spec((d, 7 * d), layer),
                  _layer_spec((1, 2 * d), layer)],
        out_specs=[pl.BlockSpec((tm, d), row), pairs, pairs, pairs, pl.BlockSpec((tm, 2 * d), row),
                   pl.BlockSpec((1, tm, d), st), pl.BlockSpec((1, tm, d), st)],
        out_shape=[jax.ShapeDtypeStruct((t, d), F32), pairs_shape, pairs_shape, pairs_shape,
                   jax.ShapeDtypeStruct((t, 2 * d), F32),
                   jax.ShapeDtypeStruct((n_seq, tm, d), F32), jax.ShapeDtypeStruct((n_seq, tm, d), F32)],
        compiler_params=_params(1),
        name="in_proj",
    )(x, norm_w, w_in, b_gate)


def _conv_kernel(u_ref, hist_ref, w_ref, b_ref, lg_ref, lb_ref, wo_ref, c_ref, xe_ref, xs_ref, y_ref, *, tm, rows):
    j = pl.program_id(1)

    @pl.when(j == 0)
    def _():
        for cb in range(D_MODEL // LANES):
            xe_ref[cb, 0:CONV_HIST] = hist_ref[:, cb * LANES:(cb + 1) * LANES]

    @pl.when(j > 0)
    def _():
        for cb in range(D_MODEL // LANES):
            xe_ref[cb, 0:CONV_HIST] = xe_ref[cb, tm:tm + CONV_HIST]

    n_shift = CONV_HIST + tm - SUBLANES
    first = CONV_HIST - (CONV_W - 1)
    for cb in range(D_MODEL // LANES):
        cs = slice(cb * LANES, (cb + 1) * LANES)
        xe_ref[cb, CONV_HIST:CONV_HIST + tm] = u_ref[:, cs]
        for r in range(1, SUBLANES):
            xs_ref[cb, r - 1] = xe_ref[cb, r:r + n_shift]

        def row_chunk(rc, carry, cb=cb, cs=cs):
            r0 = pl.multiple_of(rc * rows, rows)
            acc = jnp.broadcast_to(b_ref[:, cs], (rows, LANES))
            for tap in range(CONV_W):
                shift = (first + tap) % SUBLANES
                base = first + tap - shift
                if shift == 0:
                    win = xe_ref[cb, pl.ds(r0 + base, rows), :]
                else:
                    win = xs_ref[cb, shift - 1, pl.ds(r0 + base, rows), :]
                acc = acc + w_ref[tap:tap + 1, cs] * win
            y_ref[pl.ds(r0, rows), cs] = acc
            return carry

        lax.fori_loop(0, tm // rows, row_chunk, 0)

    y = y_ref[...]
    mu = jnp.mean(y, axis=-1, keepdims=True)
    yc = y - mu
    var = jnp.mean(yc * yc, axis=-1, keepdims=True)
    z = yc * lax.rsqrt(var + EPS) * lg_ref[...] + lb_ref[...]
    c_ref[...] = _dot(jax.nn.silu(z).astype(BF16), wo_ref[...])


def _conv_module(u, hist, hist_layer, conv_dw, conv_b, ln_g, ln_b, w_conv_out, layer, n_seq, tm, rows=64):
    t, d = u.shape
    nt = t // (n_seq * tm)
    per_seq_hist = hist.shape[1] > 1
    kern = functools.partial(_conv_kernel, tm=tm, rows=rows)
    row = lambda b, j: (b * nt + j, 0)
    return pl.pallas_call(
        kern,
        grid=(n_seq, nt),
        in_specs=[pl.BlockSpec((tm, d), row),
                  pl.BlockSpec((None, None, CONV_HIST, d), lambda b, j: (hist_layer, b if per_seq_hist else 0, 0, 0)),
                  _layer_spec((CONV_W, d), layer), _layer_spec((1, d), layer), _layer_spec((1, d), layer),
                  _layer_spec((1, d), layer), _layer_spec((d, d), layer)],
        out_specs=pl.BlockSpec((tm, d), row),
        out_shape=jax.ShapeDtypeStruct((t, d), F32),
        scratch_shapes=[pltpu.VMEM((d // LANES, CONV_HIST + tm, LANES), F32),
                        pltpu.VMEM((d // LANES, SUBLANES - 1, CONV_HIST + tm - SUBLANES, LANES), F32),
                        pltpu.VMEM((tm, d), F32)],
        compiler_params=_params(2),
        name="conv_module",
    )(u, hist, conv_dw, conv_b, ln_g, ln_b, w_conv_out)


def _bias_kernel(t_ref, bp_ref, bs_ref):
    t = t_ref[...]
    hi = t.astype(BF16)
    rest = t - hi.astype(F32)
    mid = rest.astype(BF16)
    lo = (rest - mid.astype(F32)).astype(BF16)
    idx = lax.broadcasted_iota(jnp.int32, (TABLE_PAD, K_WINDOW), 0)
    col = lax.broadcasted_iota(jnp.int32, (TABLE_PAD, K_WINDOW), 1)
    sel = jnp.where(col < ATT_PAST + CHUNK, jnp.clip(ATT_PAST - col, -REL_CLIP, REL_CLIP) + REL_CLIP, 2 * REL_CLIP)
    onehot = jnp.where(idx == sel, 1.0, 0.0).astype(BF16)
    row0 = (_dot(hi, onehot) + _dot(mid, onehot) + _dot(lo, onehot)) * LOG2E
    r = lax.broadcasted_iota(jnp.int32, (Q_BLOCK, K_WINDOW), 0)
    c = lax.broadcasted_iota(jnp.int32, (Q_BLOCK, K_WINDOW), 1)
    band = ((r < CHUNK) & (c < ATT_PAST + CHUNK)) | ((r >= CHUNK) & (c >= CHUNK))
    for h in range(N_HEADS):
        x = pltpu.roll(jnp.broadcast_to(row0[h:h + 1], (Q_BLOCK, K_WINDOW)), 0, 1, stride=1, stride_axis=0)
        x = jnp.where(band, x, NEG_INF)
        bp_ref[h] = x
        bs_ref[h] = x[:CHUNK]


def _attention_bias(rel_table):
    assert ATT_PAST >= REL_CLIP and rel_table.shape[2] == 2 * REL_CLIP + 1
    depth = rel_table.shape[0]
    table = jnp.pad(rel_table, ((0, 0), (0, 0), (0, TABLE_PAD - rel_table.shape[2])))
    bp, bs = pl.pallas_call(
        _bias_kernel,
        grid=(depth,),
        in_specs=[pl.BlockSpec((None, N_HEADS, TABLE_PAD), lambda l: (l, 0, 0))],
        out_specs=[pl.BlockSpec((None, N_HEADS, Q_BLOCK, K_WINDOW), lambda l: (l, 0, 0, 0)),
                   pl.BlockSpec((None, N_HEADS, CHUNK, K_WINDOW), lambda l: (l, 0, 0, 0))],
        out_shape=[jax.ShapeDtypeStruct((depth, N_HEADS, Q_BLOCK, K_WINDOW), F32),
                   jax.ShapeDtypeStruct((depth, N_HEADS, CHUNK, K_WINDOW), F32)],
        compiler_params=_params(1),
        name="rel_bias",
    )(table)
    return bp.reshape(depth, N_PAIRS, 2 * Q_BLOCK, K_WINDOW), bs.reshape(depth, N_PAIRS, 2 * CHUNK, K_WINDOW)


def _scores(q_blk, k_win):
    lane = lax.broadcasted_iota(jnp.int32, q_blk.shape, 1)
    zero = jnp.zeros_like(q_blk)
    qq = jnp.concatenate([jnp.where(lane < HEAD_DIM, q_blk, zero), jnp.where(lane >= HEAD_DIM, q_blk, zero)], axis=0)
    return lax.dot_general(qq, k_win, (((1,), (1,)), ((), ())), preferred_element_type=F32)


def _softmax_numerator(s):
    return jnp.exp2(s - jnp.max(s, axis=-1, keepdims=True)).astype(BF16)


def _weighted_values(e, v_win):
    qb = e.shape[0] // 2
    o = _dot(e, jnp.concatenate([v_win, jnp.ones_like(v_win)], axis=1))
    o = o[:, :LANES] / o[:, LANES:]
    lane = lax.broadcasted_iota(jnp.int32, (qb, LANES), 1)
    return jnp.where(lane < HEAD_DIM, o[:qb], o[qb:])


def _merge_residual(x_ref, g_ref, c_ref, att, wao_ref, wout_ref):
    d = D_MODEL
    merged = g_ref[:, 0:d] * c_ref[...] + g_ref[:, d:2 * d] * _dot(att, wao_ref[...])
    return x_ref[...] + _dot(merged.astype(BF16), wout_ref[...])


def _attn_prompt_kernel(q_ref, kp_ref, kc_ref, vp_ref, vc_ref, bm_ref, x_ref, g_ref, c_ref, wao_ref, wout_ref, y_ref,
                        kcat_ref, vcat_ref, s_ref, e_ref, att_ref, *, tm):
    j = pl.program_id(1)
    n_qb = tm // Q_BLOCK
    n_steps = N_PAIRS * n_qb
    for hp in range(N_PAIRS):
        kcat_ref[hp, 0:tm] = kp_ref[hp]
        kcat_ref[hp, tm:2 * tm] = kc_ref[hp]
        vcat_ref[hp, 0:tm] = vp_ref[hp]
        vcat_ref[hp, tm:2 * tm] = vc_ref[hp]
    key_idx = lax.broadcasted_iota(jnp.int32, (1, K_WINDOW), 1)

    def where(n):
        return n // n_qb, (n % n_qb) * Q_BLOCK

    def scores(n):
        hp, r0 = where(n)
        s_ref[n % PIPE] = _scores(q_ref[hp, r0:r0 + Q_BLOCK, :], kcat_ref[hp, r0:r0 + K_WINDOW, :])

    def softmax(n):
        hp, r0 = where(n)
        row_mask = jnp.where((j == 0) & (key_idx + r0 < tm), NEG_INF, 0.0).astype(F32)
        e_ref[n % PIPE] = _softmax_numerator(s_ref[n % PIPE] + bm_ref[hp] + row_mask)

    def values(n):
        hp, r0 = where(n)
        att_ref[hp, r0:r0 + Q_BLOCK, :] = _weighted_values(e_ref[n % PIPE], vcat_ref[hp, r0:r0 + K_WINDOW, :])

    lag = PIPE // 2
    for n in range(-lag, n_steps + lag):
        for stage, k in ((scores, n + lag), (softmax, n), (values, n - lag)):
            if 0 <= k < n_steps:
                stage(k)
    att = jnp.concatenate([att_ref[hp].astype(BF16) for hp in range(N_PAIRS)], axis=1)
    y_ref[...] = _merge_residual(x_ref, g_ref, c_ref, att, wao_ref, wout_ref)


def _attn_prompt(q, k, v, bias, x, g, c, w_att_out, w_out, layer, n_seq, tm=ATT_PAST):
    _, t, _ = q.shape
    d = D_MODEL
    nt = t // (n_seq * tm)
    cur = lambda b, j: (0, b * nt + j, 0)
    prev = lambda b, j: (0, b * nt + jnp.maximum(j - 1, 0), 0)
    row = lambda b, j: (b * nt + j, 0)
    blk = (N_PAIRS, tm, LANES)
    return pl.pallas_call(
        functools.partial(_attn_prompt_kernel, tm=tm),
        grid=(n_seq, nt),
        in_specs=[pl.BlockSpec(blk, cur), pl.BlockSpec(blk, prev), pl.BlockSpec(blk, cur),
                  pl.BlockSpec(blk, prev), pl.BlockSpec(blk, cur), _layer_spec(bias.shape[1:], layer),
                  pl.BlockSpec((tm, d), row), pl.BlockSpec((tm, 2 * d), row), pl.BlockSpec((tm, d), row),
                  _layer_spec((d, d), layer), _layer_spec((d, d), layer)],
        out_specs=pl.BlockSpec((tm, d), row),
        out_shape=jax.ShapeDtypeStruct((t, d), F32),
        scratch_shapes=[pltpu.VMEM((N_PAIRS, 2 * tm, LANES), BF16), pltpu.VMEM((N_PAIRS, 2 * tm, LANES), BF16),
                        pltpu.VMEM((PIPE, 2 * Q_BLOCK, K_WINDOW), F32), pltpu.VMEM((PIPE, 2 * Q_BLOCK, K_WINDOW), BF16),
                        pltpu.VMEM((N_PAIRS, tm, LANES), F32)],
        compiler_params=_params(2),
        name="attn_prompt",
    )(q, k, k, v, v, bias, x, g, c, w_att_out, w_out)


def _attn_sample_kernel(q_ref, kc_ref, kn_ref, knf_ref, vc_ref, vn_ref, vnf_ref, bm_ref, x_ref, g_ref, c_ref,
                        wao_ref, wout_ref, y_ref, ko_ref, vo_ref, kcat_ref, vcat_ref):
    l, t = ATT_PAST, CHUNK
    pad = jnp.zeros((K_WINDOW - l - t, LANES), BF16)
    for cache_ref, new_ref, new_f32_ref, out_ref, cat_ref in ((kc_ref, kn_ref, knf_ref, ko_ref, kcat_ref),
                                                              (vc_ref, vn_ref, vnf_ref, vo_ref, vcat_ref)):
        for hp in range(N_PAIRS):
            cat_ref[hp, 0:l] = cache_ref[:, hp * LANES:(hp + 1) * LANES].astype(BF16)
            cat_ref[hp, l:l + t] = new_ref[hp]
            cat_ref[hp, l + t:K_WINDOW] = pad
        out_ref[0:l - t] = cache_ref[t:l]
        out_ref[l - t:l] = new_f32_ref[...]
    att = []
    for hp in range(N_PAIRS):
        e = _softmax_numerator(_scores(q_ref[hp], kcat_ref[hp]) + bm_ref[hp])
        att.append(_weighted_values(e, vcat_ref[hp]).astype(BF16))
    y_ref[...] = _merge_residual(x_ref, g_ref, c_ref, jnp.concatenate(att, axis=1), wao_ref, wout_ref)


def _attn_sample(q, cache_k, k_new, k_new_f32, cache_v, v_new, v_new_f32, bias, x, g, c, w_att_out, w_out, layer):
    _, t, _ = q.shape
    d = D_MODEL
    n_seq = t // CHUNK
    row = lambda b: (b, 0)
    pairs = pl.BlockSpec((N_PAIRS, CHUNK, LANES), lambda b: (0, b, 0))
    cache = pl.BlockSpec((None, None, ATT_PAST, d), lambda b: (layer, b, 0, 0))
    new_cache = pl.BlockSpec((None, ATT_PAST, d), lambda b: (b, 0, 0))
    return pl.pallas_call(
        _attn_sample_kernel,
        grid=(n_seq,),
        in_specs=[pairs, cache, pairs, pl.BlockSpec((CHUNK, d), row), cache, pairs, pl.BlockSpec((CHUNK, d), row),
                  _layer_spec(bias.shape[1:], layer),
                  pl.BlockSpec((CHUNK, d), row), pl.BlockSpec((CHUNK, 2 * d), row), pl.BlockSpec((CHUNK, d), row),
                  _layer_spec((d, d), layer), _layer_spec((d, d), layer)],
        out_specs=[pl.BlockSpec((CHUNK, d), row), new_cache, new_cache],
        out_shape=[jax.ShapeDtypeStruct((t, d), F32), jax.ShapeDtypeStruct((n_seq, ATT_PAST, d), F32),
                   jax.ShapeDtypeStruct((n_seq, ATT_PAST, d), F32)],
        scratch_shapes=[pltpu.VMEM((N_PAIRS, K_WINDOW, LANES), BF16), pltpu.VMEM((N_PAIRS, K_WINDOW, LANES), BF16)],
        compiler_params=_params(1),
        name="attn_sample",
    )(q, cache_k, k_new, k_new_f32, cache_v, v_new, v_new_f32, bias, x, g, c, w_att_out, w_out)


def _ffn_kernel(x_ref, p_ref, fh_ref, nf_ref, wup_ref, fw_ref, fb_ref, wdn_ref,
                npl_ref, wpg_ref, wpp_ref, nfin_ref, y_ref, fst_ref, up_ref, *, tm, tiles_per_seq, final):
    i = pl.program_id(0)
    x = x_ref[...]
    h = _rms(x, nf_ref[...]).astype(BF16)

    @pl.when(i % tiles_per_seq == 0)
    def _():
        up_ref[0:FFN_HIST] = fh_ref[...]

    @pl.when(i % tiles_per_seq != 0)
    def _():
        up_ref[0:FFN_HIST] = up_ref[tm:tm + FFN_HIST]

    up_ref[FFN_HIST:FFN_HIST + tm] = _dot(h, wup_ref[:, 0:D_FF])
    gate = _dot(h, wup_ref[:, D_FF:2 * D_FF])
    fst_ref[0] = up_ref[tm:tm + FFN_HIST]
    cv = fb_ref[...]
    for tap in range(FFN_W):
        first = FFN_HIST - (FFN_W - 1) + tap
        cv = cv + fw_ref[tap:tap + 1] * up_ref[first:first + tm]
    act = cv * (lax.erf(cv / jnp.sqrt(F32(2.0))) + 1.0) / 2.0 * gate
    x = x + _dot(act.astype(BF16), wdn_ref[...])

    h = _rms(x, npl_ref[...]).astype(BF16)
    x = x + jax.nn.sigmoid(_dot(h, wpg_ref[...])) * _dot(p_ref[...].astype(BF16), wpp_ref[...])
    y_ref[...] = _rms(x, nfin_ref[...]) if final else x


def _ffn(x, p, ffn_hist, hist_layer, norm_ffn, w_up, ffn_dw, ffn_b, w_down, norm_ple, w_pg, w_pp,
         norm_final, layer, seg, tm, final):
    t, d = x.shape
    n_tiles = t // tm
    tiles_per_seq = seg // tm
    per_seq_hist = ffn_hist.shape[1] > 1
    row = lambda i: (i, 0)
    kern = functools.partial(_ffn_kernel, tm=tm, tiles_per_seq=tiles_per_seq, final=final)
    return pl.pallas_call(
        kern,
        grid=(n_tiles,),
        in_specs=[pl.BlockSpec((tm, d), row), pl.BlockSpec((None, tm, PLE_DIM), lambda i: (layer, i, 0)),
                  pl.BlockSpec((None, None, FFN_HIST, D_FF),
                               lambda i: (hist_layer, (i // tiles_per_seq) if per_seq_hist else 0, 0, 0)),
                  _layer_spec((1, d), layer), _layer_spec((d, 2 * D_FF), layer),
                  _layer_spec((FFN_W, D_FF), layer), _layer_spec((1, D_FF), layer), _layer_spec((D_FF, d), layer),
                  _layer_spec((1, d), layer), _layer_spec((d, d), layer), _layer_spec((PLE_DIM, d), layer),
                  _const_spec((1, d))],
        out_specs=[pl.BlockSpec((tm, d), row), pl.BlockSpec((1, FFN_HIST, D_FF), lambda i: (i // tiles_per_seq, 0, 0))],
        out_shape=[jax.ShapeDtypeStruct((t, d), F32),
                   jax.ShapeDtypeStruct((n_tiles // tiles_per_seq, FFN_HIST, D_FF), F32)],
        scratch_shapes=[pltpu.VMEM((FFN_HIST + tm, D_FF), F32)],
        compiler_params=_params(1),
        name="ffn",
    )(x, p, ffn_hist, norm_ffn, w_up, ffn_dw, ffn_b, w_down, norm_ple, w_pg, w_pp, norm_final)


def _pad_rows_front(h, rows):
    return jnp.pad(h, ((0, 0),) * (h.ndim - 2) + ((rows - h.shape[-2], 0), (0, 0)))


def kernel(x_prompt, x_sample, p_prompt, p_sample, cache_att_k, cache_att_v, state_conv, state_ffn_conv, norm_mix, w_in, conv_dw, conv_dw_b, conv_ln_g, conv_ln_b, w_conv_out, rel_table, w_att_out, b_gate, w_out, norm_ffn, w_ffn_up, ffn_dw, ffn_dw_b, w_ffn_down, norm_ple, w_ple_gate, w_ple_proj, norm_final):
    depth = w_in.shape[0]
    bp, sp, d = x_prompt.shape
    bs, ss, _ = x_sample.shape
    l_cache = cache_att_k.shape[2]
    xp = x_prompt.reshape(bp * sp, d)
    xs = x_sample.reshape(bs * ss, d)
    pp = p_prompt.reshape(depth, bp * sp, PLE_DIM)
    ps = p_sample.reshape(depth, bs * ss, PLE_DIM)
    ck = cache_att_k.reshape(depth, bs, l_cache, d)
    cv = cache_att_v.reshape(depth, bs, l_cache, d)
    rows = lambda v: v.reshape(v.shape[0], 1, v.shape[1])

    w_in_b, w_co_b, w_ao_b, w_out_b = (w.astype(BF16) for w in (w_in, w_conv_out, w_att_out, w_out))
    w_up_b, w_dn_b, w_pg_b, w_pp_b = (w.astype(BF16) for w in (w_ffn_up, w_ffn_down, w_ple_gate, w_ple_proj))
    bias_p, bias_s = _attention_bias(rel_table)
    conv_hist_s = _pad_rows_front(state_conv, CONV_HIST)
    ffn_hist_s = _pad_rows_front(state_ffn_conv, FFN_HIST)
    conv_hist_p = jnp.zeros((1, 1, CONV_HIST, d), F32)
    ffn_hist_p = jnp.zeros((1, 1, FFN_HIST, D_FF), F32)
    in_args = (rows(norm_mix), w_in_b, rows(b_gate))
    conv_args = (conv_dw, rows(conv_dw_b), rows(conv_ln_g), rows(conv_ln_b), w_co_b)
    ffn_args = (rows(norm_ffn), w_up_b, ffn_dw, rows(ffn_dw_b), w_dn_b, rows(norm_ple), w_pg_b, w_pp_b,
                norm_final.reshape(1, d))
    keep = min(ATT_PAST, sp)

    outs = [[] for _ in range(8)]
    for i in range(depth):
        final = i == depth - 1
        u, q, k, v, g, kst, vst = _in_proj(xp, *in_args, layer=i, tiles_per_seq=sp // 512)
        c = _conv_module(u, conv_hist_p, 0, *conv_args, layer=i, n_seq=bp, tm=512)
        xp = _attn_prompt(q, k, v, bias_p, xp, g, c, w_ao_b, w_out_b, layer=i, n_seq=bp)
        xp, fst = _ffn(xp, pp, ffn_hist_p, 0, *ffn_args, layer=i, seg=sp, tm=512, final=final)
        outs[0].append(kst[:, 512 - keep:])
        outs[1].append(vst[:, 512 - keep:])
        outs[2].append(u.reshape(bp, sp, d)[:, sp - (CONV_W - 1):])
        outs[3].append(fst[:, FFN_HIST - (FFN_W - 1):])

        u, q, k, v, g, kst, vst = _in_proj(xs, *in_args, layer=i, tiles_per_seq=1)
        c = _conv_module(u, conv_hist_s, i, *conv_args, layer=i, n_seq=bs, tm=ss)
        xs, k_cache, v_cache = _attn_sample(q, ck, k, kst[0], cv, v, vst[0], bias_s, xs, g, c, w_ao_b, w_out_b, layer=i)
        xs, fst = _ffn(xs, ps, ffn_hist_s, i, *ffn_args, layer=i, seg=ss, tm=ss, final=final)
        outs[4].append(k_cache)
        outs[5].append(v_cache)
        outs[6].append(jnp.concatenate([state_conv[i], u.reshape(bs, ss, d)], axis=1)[:, ss:])
        outs[7].append(fst[:, FFN_HIST - (FFN_W - 1):])

    heads = lambda o: jnp.stack(o).reshape(depth, -1, o[0].shape[1], N_HEADS, HEAD_DIM)
    return (xp.reshape(bp, sp, d), xs.reshape(bs, ss, d), heads(outs[0]), heads(outs[1]), jnp.stack(outs[2]),
            jnp.stack(outs[3]), heads(outs[4]), heads(outs[5]), jnp.stack(outs[6]), jnp.stack(outs[7]))
```

```python
import functools
import math

import jax
import jax.numpy as jnp
from jax import lax
from jax.experimental import pallas as pl
from jax.experimental.pallas import tpu as pltpu

D_MODEL = 1024
N_HEADS = 16
HEAD_DIM = 64
CHUNK = 64
N_PAST_CHUNKS = 8
ATT_PAST = N_PAST_CHUNKS * CHUNK
REL_CLIP = 128
CONV_W = 31
D_FF = 2816
FFN_W = 3
PLE_DIM = 256
EPS = 1e-6
NEG_INF = -1e30

LANES = 128
SUBLANES = 8
N_PAIRS = D_MODEL // LANES
Q_BLOCK = 2 * CHUNK
K_WINDOW = ATT_PAST + Q_BLOCK
PIPE = 4
CONV_HIST = 32
FFN_HIST = 8
TABLE_PAD = 384
VMEM_LIMIT = 60 * 1024 * 1024
LOG2E = math.log2(math.e)
Q_SCALE = HEAD_DIM ** -0.5 * LOG2E

BF16 = jnp.bfloat16
F32 = jnp.float32


def _const_spec(shape):
    zeros = (0,) * len(shape)
    return pl.BlockSpec(shape, lambda *_: zeros, pipeline_mode=pl.Buffered(1))


def _layer_spec(shape, layer):
    zeros = (0,) * len(shape)
    return pl.BlockSpec((None, *shape), lambda *_: (layer, *zeros), pipeline_mode=pl.Buffered(1))


def _params(n_axes):
    return pltpu.CompilerParams(dimension_semantics=("arbitrary",) * n_axes, vmem_limit_bytes=VMEM_LIMIT)


def _rms(x, w):
    return x * lax.rsqrt(jnp.mean(x * x, axis=-1, keepdims=True) + EPS) * w


def _dot(a, b):
    return jnp.dot(a, b, preferred_element_type=F32)


def _in_proj_kernel(x_ref, nw_ref, w_ref, bg_ref, u_ref, q_ref, k_ref, v_ref, g_ref, kst_ref, vst_ref):
    d = D_MODEL
    h = _rms(x_ref[...], nw_ref[...]).astype(BF16)
    glu_a = _dot(h, w_ref[:, 0:d])
    glu_b = _dot(h, w_ref[:, d:2 * d])
    u_ref[...] = glu_a * jax.nn.sigmoid(glu_b)

    def put_pairs(dst_ref, val):
        for hp in range(N_PAIRS):
            dst_ref[hp] = val[:, hp * LANES:(hp + 1) * LANES]

    put_pairs(q_ref, (_dot(h, w_ref[:, 2 * d:3 * d]) * Q_SCALE).astype(BF16))
    k = _dot(h, w_ref[:, 3 * d:4 * d])
    put_pairs(k_ref, k.astype(BF16))
    kst_ref[0] = k
    v = _dot(h, w_ref[:, 4 * d:5 * d])
    put_pairs(v_ref, v.astype(BF16))
    vst_ref[0] = v
    g_ref[...] = jax.nn.sigmoid(_dot(h, w_ref[:, 5 * d:7 * d]) + bg_ref[...])


def _in_proj(x, norm_w, w_in, b_gate, layer, tiles_per_seq, tm=512):
    t, d = x.shape
    n_tiles = t // tm
    n_seq = n_tiles // tiles_per_seq
    row = lambda i: (i, 0)
    st = lambda i: (i // tiles_per_seq, 0, 0)
    pairs = pl.BlockSpec((N_PAIRS, tm, LANES), lambda i: (0, i, 0))
    pairs_shape = jax.ShapeDtypeStruct((N_PAIRS, t, LANES), BF16)
    return pl.pallas_call(
        _in_proj_kernel,
        grid=(n_tiles,),
        in_specs=[pl.BlockSpec((tm, d), row), _layer_spec((1, d), layer), _layer_spec((d, 7 * d), layer),
                  _layer_spec((1, 2 * d), layer)],
        out_specs=[pl.BlockSpec((tm, d), row), pairs, pairs, pairs, pl.BlockSpec((tm, 2 * d), row),
                   pl.BlockSpec((1, tm, d), st), pl.BlockSpec((1, tm, d), st)],
        out_shape=[jax.ShapeDtypeStruct((t, d), F32), pairs_shape, pairs_shape, pairs_shape,
                   jax.ShapeDtypeStruct((t, 2 * d), F32),
                   jax.ShapeDtypeStruct((n_seq, tm, d), F32), jax.ShapeDtypeStruct((n_seq, tm, d), F32)],
        compiler_params=_params(1),
        name="in_proj",
    )(x, norm_w, w_in, b_gate)


def _conv_kernel(u_ref, hist_ref, w_ref, b_ref, lg_ref, lb_ref, wo_ref, c_ref, xe_ref, xs_ref, y_ref, *, tm, rows):
    j = pl.program_id(1)

    @pl.when(j == 0)
    def _():
        for cb in range(D_MODEL // LANES):
            xe_ref[cb, 0:CONV_HIST] = hist_ref[:, cb * LANES:(cb + 1) * LANES]

    @pl.when(j > 0)
    def _():
        for cb in range(D_MODEL // LANES):
            xe_ref[cb, 0:CONV_HIST] = xe_ref[cb, tm:tm + CONV_HIST]

    n_shift = CONV_HIST + tm - SUBLANES
    first = CONV_HIST - (CONV_W - 1)
    for cb in range(D_MODEL // LANES):
        cs = slice(cb * LANES, (cb + 1) * LANES)
        xe_ref[cb, CONV_HIST:CONV_HIST + tm] = u_ref[:, cs]
        for r in range(1, SUBLANES):
            xs_ref[cb, r - 1] = xe_ref[cb, r:r + n_shift]

        for r0 in range(0, tm, rows):
            acc = jnp.broadcast_to(b_ref[:, cs], (rows, LANES))
            for tap in range(CONV_W):
                shift = (first + tap) % SUBLANES
                base = first + tap - shift
                if shift == 0:
                    win = xe_ref[cb, r0 + base:r0 + base + rows, :]
                else:
                    win = xs_ref[cb, shift - 1, r0 + base:r0 + base + rows, :]
                acc = acc + w_ref[tap:tap + 1, cs] * win
            y_ref[r0:r0 + rows, cs] = acc

    y = y_ref[...]
    mu = jnp.mean(y, axis=-1, keepdims=True)
    yc = y - mu
    var = jnp.mean(yc * yc, axis=-1, keepdims=True)
    z = yc * lax.rsqrt(var + EPS) * lg_ref[...] + lb_ref[...]
    c_ref[...] = _dot(jax.nn.silu(z).astype(BF16), wo_ref[...])


def _conv_module(u, hist, hist_layer, conv_dw, conv_b, ln_g, ln_b, w_conv_out, layer, n_seq, tm, rows=64):
    t, d = u.shape
    nt = t // (n_seq * tm)
    per_seq_hist = hist.shape[1] > 1
    kern = functools.partial(_conv_kernel, tm=tm, rows=rows)
    row = lambda b, j: (b * nt + j, 0)
    return pl.pallas_call(
        kern,
        grid=(n_seq, nt),
        in_specs=[pl.BlockSpec((tm, d), row),
                  pl.BlockSpec((None, None, CONV_HIST, d), lambda b, j: (hist_layer, b if per_seq_hist else 0, 0, 0)),
                  _layer_spec((CONV_W, d), layer), _layer_spec((1, d), layer), _layer_spec((1, d), layer),
                  _layer_spec((1, d), layer), _layer_spec((d, d), layer)],
        out_specs=pl.BlockSpec((tm, d), row),
        out_shape=jax.ShapeDtypeStruct((t, d), F32),
        scratch_shapes=[pltpu.VMEM((d // LANES, CONV_HIST + tm, LANES), F32),
                        pltpu.VMEM((d // LANES, SUBLANES - 1, CONV_HIST + tm - SUBLANES, LANES), F32),
                        pltpu.VMEM((tm, d), F32)],
        compiler_params=_params(2),
        name="conv_module",
    )(u, hist, conv_dw, conv_b, ln_g, ln_b, w_conv_out)


def _bias_kernel(t_ref, bp_ref, bs_ref):
    t = t_ref[...]
    hi = t.astype(BF16)
    rest = t - hi.astype(F32)
    mid = rest.astype(BF16)
    lo = (rest - mid.astype(F32)).astype(BF16)
    idx = lax.broadcasted_iota(jnp.int32, (TABLE_PAD, K_WINDOW), 0)
    col = lax.broadcasted_iota(jnp.int32, (TABLE_PAD, K_WINDOW), 1)
    sel = jnp.where(col < ATT_PAST + CHUNK, jnp.clip(ATT_PAST - col, -REL_CLIP, REL_CLIP) + REL_CLIP, 2 * REL_CLIP)
    onehot = jnp.where(idx == sel, 1.0, 0.0).astype(BF16)
    row0 = (_dot(hi, onehot) + _dot(mid, onehot) + _dot(lo, onehot)) * LOG2E
    r = lax.broadcasted_iota(jnp.int32, (Q_BLOCK, K_WINDOW), 0)
    c = lax.broadcasted_iota(jnp.int32, (Q_BLOCK, K_WINDOW), 1)
    band = ((r < CHUNK) & (c < ATT_PAST + CHUNK)) | ((r >= CHUNK) & (c >= CHUNK))
    for h in range(N_HEADS):
        x = pltpu.roll(jnp.broadcast_to(row0[h:h + 1], (Q_BLOCK, K_WINDOW)), 0, 1, stride=1, stride_axis=0)
        x = jnp.where(band, x, NEG_INF)
        bp_ref[h] = x
        bs_ref[h] = x[:CHUNK]


def _attention_bias(rel_table):
    assert ATT_PAST >= REL_CLIP and rel_table.shape[2] == 2 * REL_CLIP + 1
    depth = rel_table.shape[0]
    table = jnp.pad(rel_table, ((0, 0), (0, 0), (0, TABLE_PAD - rel_table.shape[2])))
    bp, bs = pl.pallas_call(
        _bias_kernel,
        grid=(depth,),
        in_specs=[pl.BlockSpec((None, N_HEADS, TABLE_PAD), lambda l: (l, 0, 0))],
        out_specs=[pl.BlockSpec((None, N_HEADS, Q_BLOCK, K_WINDOW), lambda l: (l, 0, 0, 0)),
                   pl.BlockSpec((None, N_HEADS, CHUNK, K_WINDOW), lambda l: (l, 0, 0, 0))],
        out_shape=[jax.ShapeDtypeStruct((depth, N_HEADS, Q_BLOCK, K_WINDOW), F32),
                   jax.ShapeDtypeStruct((depth, N_HEADS, CHUNK, K_WINDOW), F32)],
        compiler_params=_params(1),
        name="rel_bias",
    )(table)
    return bp.reshape(depth, N_PAIRS, 2 * Q_BLOCK, K_WINDOW), bs.reshape(depth, N_PAIRS, 2 * CHUNK, K_WINDOW)


def _scores(q_blk, k_win):
    lane = lax.broadcasted_iota(jnp.int32, q_blk.shape, 1)
    zero = jnp.zeros_like(q_blk)
    qq = jnp.concatenate([jnp.where(lane < HEAD_DIM, q_blk, zero), jnp.where(lane >= HEAD_DIM, q_blk, zero)], axis=0)
    return lax.dot_general(qq, k_win, (((1,), (1,)), ((), ())), preferred_element_type=F32)


def _softmax_numerator(s):
    return jnp.exp2(s - jnp.max(s, axis=-1, keepdims=True)).astype(BF16)


def _weighted_values(e, v_win):
    qb = e.shape[0] // 2
    o = _dot(e, jnp.concatenate([v_win, jnp.ones_like(v_win)], axis=1))
    o = o[:, :LANES] / o[:, LANES:]
    lane = lax.broadcasted_iota(jnp.int32, (qb, LANES), 1)
    return jnp.where(lane < HEAD_DIM, o[:qb], o[qb:])


def _merge_residual(x_ref, g_ref, c_ref, att, wao_ref, wout_ref):
    d = D_MODEL
    merged = g_ref[:, 0:d] * c_ref[...] + g_ref[:, d:2 * d] * _dot(att, wao_ref[...])
    return x_ref[...] + _dot(merged.astype(BF16), wout_ref[...])


def _attn_prompt_kernel(q_ref, kp_ref, kc_ref, vp_ref, vc_ref, bm_ref, x_ref, g_ref, c_ref, wao_ref, wout_ref, y_ref,
                        kcat_ref, vcat_ref, s_ref, e_ref, att_ref, *, tm):
    j = pl.program_id(1)
    n_qb = tm // Q_BLOCK
    n_steps = N_PAIRS * n_qb
    for hp in range(N_PAIRS):
        kcat_ref[hp, 0:tm] = kp_ref[hp]
        kcat_ref[hp, tm:2 * tm] = kc_ref[hp]
        vcat_ref[hp, 0:tm] = vp_ref[hp]
        vcat_ref[hp, tm:2 * tm] = vc_ref[hp]
    key_idx = lax.broadcasted_iota(jnp.int32, (1, K_WINDOW), 1)

    def where(n):
        return n // n_qb, (n % n_qb) * Q_BLOCK

    def scores(n):
        hp, r0 = where(n)
        s_ref[n % PIPE] = _scores(q_ref[hp, r0:r0 + Q_BLOCK, :], kcat_ref[hp, r0:r0 + K_WINDOW, :])

    def softmax(n):
        hp, r0 = where(n)
        row_mask = jnp.where((j == 0) & (key_idx + r0 < tm), NEG_INF, 0.0).astype(F32)
        e_ref[n % PIPE] = _softmax_numerator(s_ref[n % PIPE] + bm_ref[hp] + row_mask)

    def values(n):
        hp, r0 = where(n)
        att_ref[hp, r0:r0 + Q_BLOCK, :] = _weighted_values(e_ref[n % PIPE], vcat_ref[hp, r0:r0 + K_WINDOW, :])

    lag = PIPE // 2
    for n in range(-lag, n_steps + lag):
        for stage, k in ((scores, n + lag), (softmax, n), (values, n - lag)):
            if 0 <= k < n_steps:
                stage(k)
    att = jnp.concatenate([att_ref[hp].astype(BF16) for hp in range(N_PAIRS)], axis=1)
    y_ref[...] = _merge_residual(x_ref, g_ref, c_ref, att, wao_ref, wout_ref)


def _attn_prompt(q, k, v, bias, x, g, c, w_att_out, w_out, layer, n_seq, tm=ATT_PAST):
    _, t, _ = q.shape
    d = D_MODEL
    nt = t // (n_seq * tm)
    cur = lambda b, j: (0, b * nt + j, 0)
    prev = lambda b, j: (0, b * nt + jnp.maximum(j - 1, 0), 0)
    row = lambda b, j: (b * nt + j, 0)
    blk = (N_PAIRS, tm, LANES)
    return pl.pallas_call(
        functools.partial(_attn_prompt_kernel, tm=tm),
        grid=(n_seq, nt),
        in_specs=[pl.BlockSpec(blk, cur), pl.BlockSpec(blk, prev), pl.BlockSpec(blk, cur),
                  pl.BlockSpec(blk, prev), pl.BlockSpec(blk, cur), _layer_spec(bias.shape[1:], layer),
                  pl.BlockSpec((tm, d), row), pl.BlockSpec((tm, 2 * d), row), pl.BlockSpec((tm, d), row),
                  _layer_spec((d, d), layer), _layer_spec((d, d), layer)],
        out_specs=pl.BlockSpec((tm, d), row),
        out_shape=jax.ShapeDtypeStruct((t, d), F32),
        scratch_shapes=[pltpu.VMEM((N_PAIRS, 2 * tm, LANES), BF16), pltpu.VMEM((N_PAIRS, 2 * tm, LANES), BF16),
                        pltpu.VMEM((PIPE, 2 * Q_BLOCK, K_WINDOW), F32), pltpu.VMEM((PIPE, 2 * Q_BLOCK, K_WINDOW), BF16),
                        pltpu.VMEM((N_PAIRS, tm, LANES), F32)],
        compiler_params=_params(2),
        name="attn_prompt",
    )(q, k, k, v, v, bias, x, g, c, w_att_out, w_out)


def _attn_sample_kernel(q_ref, kc_ref, kn_ref, knf_ref, vc_ref, vn_ref, vnf_ref, bm_ref, x_ref, g_ref, c_ref,
                        wao_ref, wout_ref, y_ref, ko_ref, vo_ref, kcat_ref, vcat_ref):
    l, t = ATT_PAST, CHUNK
    pad = jnp.zeros((K_WINDOW - l - t, LANES), BF16)
    for cache_ref, new_ref, new_f32_ref, out_ref, cat_ref in ((kc_ref, kn_ref, knf_ref, ko_ref, kcat_ref),
                                                              (vc_ref, vn_ref, vnf_ref, vo_ref, vcat_ref)):
        for hp in range(N_PAIRS):
            cat_ref[hp, 0:l] = cache_ref[:, hp * LANES:(hp + 1) * LANES].astype(BF16)
            cat_ref[hp, l:l + t] = new_ref[hp]
            cat_ref[hp, l + t:K_WINDOW] = pad
        out_ref[0:l - t] = cache_ref[t:l]
        out_ref[l - t:l] = new_f32_ref[...]
    att = []
    for hp in range(N_PAIRS):
        e = _softmax_numerator(_scores(q_ref[hp], kcat_ref[hp]) + bm_ref[hp])
        att.append(_weighted_values(e, vcat_ref[hp]).astype(BF16))
    y_ref[...] = _merge_residual(x_ref, g_ref, c_ref, jnp.concatenate(att, axis=1), wao_ref, wout_ref)


def _attn_sample(q, cache_k, k_new, k_new_f32, cache_v, v_new, v_new_f32, bias, x, g, c, w_att_out, w_out, layer):
    _, t, _ = q.shape
    d = D_MODEL
    n_seq = t // CHUNK
    row = lambda b: (b, 0)
    pairs = pl.BlockSpec((N_PAIRS, CHUNK, LANES), lambda b: (0, b, 0))
    cache = pl.BlockSpec((None, None, ATT_PAST, d), lambda b: (layer, b, 0, 0))
    new_cache = pl.BlockSpec((None, ATT_PAST, d), lambda b: (b, 0, 0))
    return pl.pallas_call(
        _attn_sample_kernel,
        grid=(n_seq,),
        in_specs=[pairs, cache, pairs, pl.BlockSpec((CHUNK, d), row), cache, pairs, pl.BlockSpec((CHUNK, d), row),
                  _layer_spec(bias.shape[1:], layer),
                  pl.BlockSpec((CHUNK, d), row), pl.BlockSpec((CHUNK, 2 * d), row), pl.BlockSpec((CHUNK, d), row),
                  _layer_spec((d, d), layer), _layer_spec((d, d), layer)],
        out_specs=[pl.BlockSpec((CHUNK, d), row), new_cache, new_cache],
        out_shape=[jax.ShapeDtypeStruct((t, d), F32), jax.ShapeDtypeStruct((n_seq, ATT_PAST, d), F32),
                   jax.ShapeDtypeStruct((n_seq, ATT_PAST, d), F32)],
        scratch_shapes=[pltpu.VMEM((N_PAIRS, K_WINDOW, LANES), BF16), pltpu.VMEM((N_PAIRS, K_WINDOW, LANES), BF16)],
        compiler_params=_params(1),
        name="attn_sample",
    )(q, cache_k, k_new, k_new_f32, cache_v, v_new, v_new_f32, bias, x, g, c, w_att_out, w_out)


def _ffn_kernel(x_ref, p_ref, fh_ref, nf_ref, wup_ref, fw_ref, fb_ref, wdn_ref,
                npl_ref, wpg_ref, wpp_ref, nfin_ref, y_ref, fst_ref, up_ref, *, tm, seg_rows, tiles_per_seq, final):
    i = pl.program_id(0)
    x = x_ref[...]
    h = _rms(x, nf_ref[...]).astype(BF16)
    seqs = tm // seg_rows
    stride = FFN_HIST + seg_rows

    if seqs == 1:
        @pl.when(i % tiles_per_seq == 0)
        def _():
            up_ref[0:FFN_HIST] = fh_ref[0]

        @pl.when(i % tiles_per_seq != 0)
        def _():
            up_ref[0:FFN_HIST] = up_ref[tm:tm + FFN_HIST]
    else:
        for s in range(seqs):
            up_ref[s * stride:s * stride + FFN_HIST] = fh_ref[s]

    up = _dot(h, wup_ref[:, 0:D_FF])
    for s in range(seqs):
        up_ref[s * stride + FFN_HIST:(s + 1) * stride] = up[s * seg_rows:(s + 1) * seg_rows]
    gate = _dot(h, wup_ref[:, D_FF:2 * D_FF])
    cvs = []
    for s in range(seqs):
        fst_ref[s] = up_ref[(s + 1) * stride - FFN_HIST:(s + 1) * stride]
        cv = fb_ref[...]
        for tap in range(FFN_W):
            first = s * stride + FFN_HIST - (FFN_W - 1) + tap
            cv = cv + fw_ref[tap:tap + 1] * up_ref[first:first + seg_rows]
        cvs.append(cv)
    cv = cvs[0] if seqs == 1 else jnp.concatenate(cvs, axis=0)
    act = cv * (lax.erf(cv / jnp.sqrt(F32(2.0))) + 1.0) / 2.0 * gate
    x = x + _dot(act.astype(BF16), wdn_ref[...])

    h = _rms(x, npl_ref[...]).astype(BF16)
    x = x + jax.nn.sigmoid(_dot(h, wpg_ref[...])) * _dot(p_ref[...].astype(BF16), wpp_ref[...])
    y_ref[...] = _rms(x, nfin_ref[...]) if final else x


def _ffn(x, p, ffn_hist, hist_layer, norm_ffn, w_up, ffn_dw, ffn_b, w_down, norm_ple, w_pg, w_pp,
         norm_final, layer, seg, tm, final):
    t, d = x.shape
    n_tiles = t // tm
    seg_rows = min(seg, tm)
    seqs = tm // seg_rows
    tiles_per_seq = seg // seg_rows
    per_seq_hist = ffn_hist.shape[1] > 1
    row = lambda i: (i, 0)
    seq_blk = lambda i: (i // tiles_per_seq, 0, 0)
    kern = functools.partial(_ffn_kernel, tm=tm, seg_rows=seg_rows, tiles_per_seq=tiles_per_seq, final=final)
    return pl.pallas_call(
        kern,
        grid=(n_tiles,),
        in_specs=[pl.BlockSpec((tm, d), row), pl.BlockSpec((None, tm, PLE_DIM), lambda i: (layer, i, 0)),
                  pl.BlockSpec((None, seqs, FFN_HIST, D_FF),
                               lambda i: (hist_layer, (i // tiles_per_seq) if per_seq_hist else 0, 0, 0)),
                  _layer_spec((1, d), layer), _layer_spec((d, 2 * D_FF), layer),
                  _layer_spec((FFN_W, D_FF), layer), _layer_spec((1, D_FF), layer), _layer_spec((D_FF, d), layer),
                  _layer_spec((1, d), layer), _layer_spec((d, d), layer), _layer_spec((PLE_DIM, d), layer),
                  _const_spec((1, d))],
        out_specs=[pl.BlockSpec((tm, d), row), pl.BlockSpec((seqs, FFN_HIST, D_FF), seq_blk)],
        out_shape=[jax.ShapeDtypeStruct((t, d), F32),
                   jax.ShapeDtypeStruct((n_tiles * seqs // tiles_per_seq, FFN_HIST, D_FF), F32)],
        scratch_shapes=[pltpu.VMEM((seqs * (FFN_HIST + seg_rows), D_FF), F32)],
        compiler_params=_params(1),
        name="ffn",
    )(x, p, ffn_hist, norm_ffn, w_up, ffn_dw, ffn_b, w_down, norm_ple, w_pg, w_pp, norm_final)


def _pad_rows_front(h, rows):
    return jnp.pad(h, ((0, 0),) * (h.ndim - 2) + ((rows - h.shape[-2], 0), (0, 0)))


def kernel(x_prompt, x_sample, p_prompt, p_sample, cache_att_k, cache_att_v, state_conv, state_ffn_conv, norm_mix, w_in, conv_dw, conv_dw_b, conv_ln_g, conv_ln_b, w_conv_out, rel_table, w_att_out, b_gate, w_out, norm_ffn, w_ffn_up, ffn_dw, ffn_dw_b, w_ffn_down, norm_ple, w_ple_gate, w_ple_proj, norm_final):
    depth = w_in.shape[0]
    bp, sp, d = x_prompt.shape
    bs, ss, _ = x_sample.shape
    l_cache = cache_att_k.shape[2]
    xp = x_prompt.reshape(bp * sp, d)
    xs = x_sample.reshape(bs * ss, d)
    pp = p_prompt.reshape(depth, bp * sp, PLE_DIM)
    ps = p_sample.reshape(depth, bs * ss, PLE_DIM)
    ck = cache_att_k.reshape(depth, bs, l_cache, d)
    cv = cache_att_v.reshape(depth, bs, l_cache, d)
    rows = lambda v: v.reshape(v.shape[0], 1, v.shape[1])

    w_in_b, w_co_b, w_ao_b, w_out_b = (w.astype(BF16) for w in (w_in, w_conv_out, w_att_out, w_out))
    w_up_b, w_dn_b, w_pg_b, w_pp_b = (w.astype(BF16) for w in (w_ffn_up, w_ffn_down, w_ple_gate, w_ple_proj))
    bias_p, bias_s = _attention_bias(rel_table)
    conv_hist_s = _pad_rows_front(state_conv, CONV_HIST)
    ffn_hist_s = _pad_rows_front(state_ffn_conv, FFN_HIST)
    conv_hist_p = jnp.zeros((1, 1, CONV_HIST, d), F32)
    ffn_hist_p = jnp.zeros((1, 1, FFN_HIST, D_FF), F32)
    in_args = (rows(norm_mix), w_in_b, rows(b_gate))
    conv_args = (conv_dw, rows(conv_dw_b), rows(conv_ln_g), rows(conv_ln_b), w_co_b)
    ffn_args = (rows(norm_ffn), w_up_b, ffn_dw, rows(ffn_dw_b), w_dn_b, rows(norm_ple), w_pg_b, w_pp_b,
                norm_final.reshape(1, d))
    keep = min(ATT_PAST, sp)

    outs = [[] for _ in range(8)]
    for i in range(depth):
        final = i == depth - 1
        u, q, k, v, g, kst, vst = _in_proj(xp, *in_args, layer=i, tiles_per_seq=sp // 512)
        c = _conv_module(u, conv_hist_p, 0, *conv_args, layer=i, n_seq=bp, tm=512)
        xp = _attn_prompt(q, k, v, bias_p, xp, g, c, w_ao_b, w_out_b, layer=i, n_seq=bp)
        xp, fst = _ffn(xp, pp, ffn_hist_p, 0, *ffn_args, layer=i, seg=sp, tm=512, final=final)
        outs[0].append(kst[:, 512 - keep:])
        outs[1].append(vst[:, 512 - keep:])
        outs[2].append(u.reshape(bp, sp, d)[:, sp - (CONV_W - 1):])
        outs[3].append(fst[:, FFN_HIST - (FFN_W - 1):])

        u, q, k, v, g, kst, vst = _in_proj(xs, *in_args, layer=i, tiles_per_seq=1)
        c = _conv_module(u, conv_hist_s, i, *conv_args, layer=i, n_seq=bs, tm=ss)
        xs, k_cache, v_cache = _attn_sample(q, ck, k, kst[0], cv, v, vst[0], bias_s, xs, g, c, w_ao_b, w_out_b, layer=i)
        xs, fst = _ffn(xs, ps, ffn_hist_s, i, *ffn_args, layer=i, seg=ss, tm=bs * ss, final=final)
        outs[4].append(k_cache)
        outs[5].append(v_cache)
        outs[6].append(jnp.concatenate([state_conv[i], u.reshape(bs, ss, d)], axis=1)[:, ss:])
        outs[7].append(fst[:, FFN_HIST - (FFN_W - 1):])

    heads = lambda o: jnp.stack(o).reshape(depth, -1, o[0].shape[1], N_HEADS, HEAD_DIM)
    return (xp.reshape(bp, sp, d), xs.reshape(bs, ss, d), heads(outs[0]), heads(outs[1]), jnp.stack(outs[2]),
            jnp.stack(outs[3]), heads(outs[4]), heads(outs[5]), jnp.stack(outs[6]), jnp.stack(outs[7]))
```

```python
import functools
import math

import jax
import jax.numpy as jnp
from jax import lax
from jax.experimental import pallas as pl
from jax.experimental.pallas import tpu as pltpu

D_MODEL = 1024
N_HEADS = 16
HEAD_DIM = 64
CHUNK = 64
N_PAST_CHUNKS = 8
ATT_PAST = N_PAST_CHUNKS * CHUNK
REL_CLIP = 128
CONV_W = 31
D_FF = 2816
FFN_W = 3
PLE_DIM = 256
EPS = 1e-6
NEG_INF = -1e30

LANES = 128
SUBLANES = 8
N_PAIRS = D_MODEL // LANES
Q_BLOCK = 2 * CHUNK
K_WINDOW = ATT_PAST + Q_BLOCK
PIPE = 4
CONV_HIST = 32
FFN_HIST = 8
TABLE_PAD = 384
VMEM_LIMIT = 60 * 1024 * 1024
LOG2E = math.log2(math.e)
Q_SCALE = HEAD_DIM ** -0.5 * LOG2E

BF16 = jnp.bfloat16
F32 = jnp.float32


def _const_spec(shape):
    zeros = (0,) * len(shape)
    return pl.BlockSpec(shape, lambda *_: zeros, pipeline_mode=pl.Buffered(1))


def _layer_spec(shape, layer):
    zeros = (0,) * len(shape)
    return pl.BlockSpec((None, *shape), lambda *_: (layer, *zeros), pipeline_mode=pl.Buffered(1))


def _params(n_axes):
    return pltpu.CompilerParams(dimension_semantics=("arbitrary",) * n_axes, vmem_limit_bytes=VMEM_LIMIT)


def _rms(x, w):
    return x * lax.rsqrt(jnp.mean(x * x, axis=-1, keepdims=True) + EPS) * w


def _dot(a, b):
    return jnp.dot(a, b, preferred_element_type=F32)


def _in_proj_kernel(x_ref, nw_ref, w_ref, bg_ref, *rest, tiles_per_seq, v_transposed):
    if v_transposed:
        wvt_ref, u_ref, q_ref, k_ref, v_ref, g_ref, kst_ref, vst_ref = rest
    else:
        u_ref, q_ref, k_ref, v_ref, g_ref, kst_ref, vst_ref = rest
    d = D_MODEL
    h = _rms(x_ref[...], nw_ref[...]).astype(BF16)
    glu_a = _dot(h, w_ref[:, 0:d])
    glu_b = _dot(h, w_ref[:, d:2 * d])
    u_ref[...] = glu_a * jax.nn.sigmoid(glu_b)

    def put_pairs(dst_ref, val):
        for hp in range(N_PAIRS):
            dst_ref[hp] = val[:, hp * LANES:(hp + 1) * LANES]

    put_pairs(q_ref, (_dot(h, w_ref[:, 2 * d:3 * d]) * Q_SCALE).astype(BF16))
    k = _dot(h, w_ref[:, 3 * d:4 * d])
    put_pairs(k_ref, k.astype(BF16))
    kst_ref[0] = k
    if v_transposed:
        vt = lax.dot_general(wvt_ref[...], h, (((1,), (1,)), ((), ())), preferred_element_type=F32).astype(BF16)
        for hp in range(N_PAIRS):
            v_ref[hp] = vt[hp * LANES:(hp + 1) * LANES, :]

        @pl.when(pl.program_id(0) % tiles_per_seq == tiles_per_seq - 1)
        def _():
            vst_ref[0] = _dot(h, w_ref[:, 4 * d:5 * d])
    else:
        v = _dot(h, w_ref[:, 4 * d:5 * d])
        put_pairs(v_ref, v.astype(BF16))
        vst_ref[0] = v
    g_ref[...] = jax.nn.sigmoid(_dot(h, w_ref[:, 5 * d:7 * d]) + bg_ref[...])


def _in_proj(x, norm_w, w_in, b_gate, layer, tiles_per_seq, tm=512, w_vt=None):
    t, d = x.shape
    n_tiles = t // tm
    n_seq = n_tiles // tiles_per_seq
    row = lambda i: (i, 0)
    st = lambda i: (i // tiles_per_seq, 0, 0)
    pairs = pl.BlockSpec((N_PAIRS, tm, LANES), lambda i: (0, i, 0))
    pairs_shape = jax.ShapeDtypeStruct((N_PAIRS, t, LANES), BF16)
    vt = w_vt is not None
    v_spec = pl.BlockSpec((N_PAIRS, LANES, tm), lambda i: (0, 0, i)) if vt else pairs
    v_shape = jax.ShapeDtypeStruct((N_PAIRS, LANES, t), BF16) if vt else pairs_shape
    return pl.pallas_call(
        functools.partial(_in_proj_kernel, tiles_per_seq=tiles_per_seq, v_transposed=vt),
        grid=(n_tiles,),
        in_specs=[pl.BlockSpec((tm, d), row), _layer_spec((1, d), layer), _layer_spec((d, 7 * d), layer),
                  _layer_spec((1, 2 * d), layer)] + ([_layer_spec((d, d), layer)] if vt else []),
        out_specs=[pl.BlockSpec((tm, d), row), pairs, pairs, v_spec, pl.BlockSpec((tm, 2 * d), row),
                   pl.BlockSpec((1, tm, d), st), pl.BlockSpec((1, tm, d), st)],
        out_shape=[jax.ShapeDtypeStruct((t, d), F32), pairs_shape, pairs_shape, v_shape,
                   jax.ShapeDtypeStruct((t, 2 * d), F32),
                   jax.ShapeDtypeStruct((n_seq, tm, d), F32), jax.ShapeDtypeStruct((n_seq, tm, d), F32)],
        compiler_params=_params(1),
        name="in_proj",
    )(x, norm_w, w_in, b_gate, *([w_vt] if vt else []))


def _conv_kernel(u_ref, hist_ref, w_ref, b_ref, lg_ref, lb_ref, wo_ref, c_ref, xe_ref, xs_ref, y_ref, *, tm, rows):
    j = pl.program_id(1)

    @pl.when(j == 0)
    def _():
        for cb in range(D_MODEL // LANES):
            xe_ref[cb, 0:CONV_HIST] = hist_ref[:, cb * LANES:(cb + 1) * LANES]

    @pl.when(j > 0)
    def _():
        for cb in range(D_MODEL // LANES):
            xe_ref[cb, 0:CONV_HIST] = xe_ref[cb, tm:tm + CONV_HIST]

    n_shift = CONV_HIST + tm - SUBLANES
    first = CONV_HIST - (CONV_W - 1)
    for cb in range(D_MODEL // LANES):
        cs = slice(cb * LANES, (cb + 1) * LANES)
        xe_ref[cb, CONV_HIST:CONV_HIST + tm] = u_ref[:, cs]
        for r in range(1, SUBLANES):
            xs_ref[cb, r - 1] = xe_ref[cb, r:r + n_shift]

        for r0 in range(0, tm, rows):
            acc = jnp.broadcast_to(b_ref[:, cs], (rows, LANES))
            for tap in range(CONV_W):
                shift = (first + tap) % SUBLANES
                base = first + tap - shift
                if shift == 0:
                    win = xe_ref[cb, r0 + base:r0 + base + rows, :]
                else:
                    win = xs_ref[cb, shift - 1, r0 + base:r0 + base + rows, :]
                acc = acc + w_ref[tap:tap + 1, cs] * win
            y_ref[r0:r0 + rows, cs] = acc

    y = y_ref[...]
    mu = jnp.mean(y, axis=-1, keepdims=True)
    yc = y - mu
    var = jnp.mean(yc * yc, axis=-1, keepdims=True)
    z = yc * lax.rsqrt(var + EPS) * lg_ref[...] + lb_ref[...]
    c_ref[...] = _dot(jax.nn.silu(z).astype(BF16), wo_ref[...])


def _conv_module(u, hist, hist_layer, conv_dw, conv_b, ln_g, ln_b, w_conv_out, layer, n_seq, tm, rows=64):
    t, d = u.shape
    nt = t // (n_seq * tm)
    per_seq_hist = hist.shape[1] > 1
    kern = functools.partial(_conv_kernel, tm=tm, rows=rows)
    row = lambda b, j: (b * nt + j, 0)
    return pl.pallas_call(
        kern,
        grid=(n_seq, nt),
        in_specs=[pl.BlockSpec((tm, d), row),
                  pl.BlockSpec((None, None, CONV_HIST, d), lambda b, j: (hist_layer, b if per_seq_hist else 0, 0, 0)),
                  _layer_spec((CONV_W, d), layer), _layer_spec((1, d), layer), _layer_spec((1, d), layer),
                  _layer_spec((1, d), layer), _layer_spec((d, d), layer)],
        out_specs=pl.BlockSpec((tm, d), row),
        out_shape=jax.ShapeDtypeStruct((t, d), F32),
        scratch_shapes=[pltpu.VMEM((d // LANES, CONV_HIST + tm, LANES), F32),
                        pltpu.VMEM((d // LANES, SUBLANES - 1, CONV_HIST + tm - SUBLANES, LANES), F32),
                        pltpu.VMEM((tm, d), F32)],
        compiler_params=_params(2),
        name="conv_module",
    )(u, hist, conv_dw, conv_b, ln_g, ln_b, w_conv_out)


def _bias_kernel(t_ref, bt_ref, bs_ref):
    t = t_ref[...]
    hi = t.astype(BF16)
    rest = t - hi.astype(F32)
    mid = rest.astype(BF16)
    lo = (rest - mid.astype(F32)).astype(BF16)
    idx = lax.broadcasted_iota(jnp.int32, (TABLE_PAD, K_WINDOW), 0)
    col = lax.broadcasted_iota(jnp.int32, (TABLE_PAD, K_WINDOW), 1)
    sel = jnp.where(col < ATT_PAST + CHUNK, jnp.clip(ATT_PAST - col, -REL_CLIP, REL_CLIP) + REL_CLIP, 2 * REL_CLIP)
    onehot = jnp.where(idx == sel, 1.0, 0.0).astype(BF16)
    row0 = (_dot(hi, onehot) + _dot(mid, onehot) + _dot(lo, onehot)) * LOG2E
    r = lax.broadcasted_iota(jnp.int32, (Q_BLOCK, K_WINDOW), 0)
    c = lax.broadcasted_iota(jnp.int32, (Q_BLOCK, K_WINDOW), 1)
    band = ((r < CHUNK) & (c < ATT_PAST + CHUNK)) | ((r >= CHUNK) & (c >= CHUNK))
    for h in range(N_HEADS):
        x = pltpu.roll(jnp.broadcast_to(row0[h:h + 1], (Q_BLOCK, K_WINDOW)), 0, 1, stride=1, stride_axis=0)
        x = jnp.where(band, x, NEG_INF)
        bt_ref[h // 2, :, (h % 2) * Q_BLOCK:(h % 2 + 1) * Q_BLOCK] = x.T
        bs_ref[h] = x[:CHUNK]


def _attention_bias(rel_table):
    assert ATT_PAST >= REL_CLIP and rel_table.shape[2] == 2 * REL_CLIP + 1
    depth = rel_table.shape[0]
    table = jnp.pad(rel_table, ((0, 0), (0, 0), (0, TABLE_PAD - rel_table.shape[2])))
    bt, bs = pl.pallas_call(
        _bias_kernel,
        grid=(depth,),
        in_specs=[pl.BlockSpec((None, N_HEADS, TABLE_PAD), lambda l: (l, 0, 0))],
        out_specs=[pl.BlockSpec((None, N_PAIRS, K_WINDOW, 2 * Q_BLOCK), lambda l: (l, 0, 0, 0)),
                   pl.BlockSpec((None, N_HEADS, CHUNK, K_WINDOW), lambda l: (l, 0, 0, 0))],
        out_shape=[jax.ShapeDtypeStruct((depth, N_PAIRS, K_WINDOW, 2 * Q_BLOCK), F32),
                   jax.ShapeDtypeStruct((depth, N_HEADS, CHUNK, K_WINDOW), F32)],
        compiler_params=_params(1),
        name="rel_bias",
    )(table)
    return bt, bs.reshape(depth, N_PAIRS, 2 * CHUNK, K_WINDOW)


def _stack_heads(q_blk):
    lane = lax.broadcasted_iota(jnp.int32, q_blk.shape, 1)
    zero = jnp.zeros_like(q_blk)
    return jnp.concatenate([jnp.where(lane < HEAD_DIM, q_blk, zero), jnp.where(lane >= HEAD_DIM, q_blk, zero)], axis=0)


def _scores(q_blk, k_win):
    return lax.dot_general(_stack_heads(q_blk), k_win, (((1,), (1,)), ((), ())), preferred_element_type=F32)


def _softmax_numerator(s):
    return jnp.exp2(s - jnp.max(s, axis=-1, keepdims=True)).astype(BF16)


def _weighted_values(e, v_win):
    qb = e.shape[0] // 2
    o = _dot(e, jnp.concatenate([v_win, jnp.ones_like(v_win)], axis=1))
    o = o[:, :LANES] / o[:, LANES:]
    lane = lax.broadcasted_iota(jnp.int32, (qb, LANES), 1)
    return jnp.where(lane < HEAD_DIM, o[:qb], o[qb:])


def _merge_residual(x_ref, g_ref, c_ref, att, wao_ref, wout_ref):
    d = D_MODEL
    merged = g_ref[:, 0:d] * c_ref[...] + g_ref[:, d:2 * d] * _dot(att, wao_ref[...])
    return x_ref[...] + _dot(merged.astype(BF16), wout_ref[...])


def _attn_prompt_kernel(q_ref, kp_ref, kc_ref, vp_ref, vc_ref, bm_ref, x_ref, g_ref, c_ref, wao_ref, wout_ref, y_ref,
                        kcat_ref, vcat_ref, s_ref, e_ref, att_ref, *, tm):
    j = pl.program_id(1)
    n_qb = tm // Q_BLOCK
    n_steps = N_PAIRS * n_qb
    for hp in range(N_PAIRS):
        kcat_ref[hp, 0:tm] = kp_ref[hp]
        kcat_ref[hp, tm:2 * tm] = kc_ref[hp]
        vcat_ref[hp, :, 0:tm] = vp_ref[hp]
        vcat_ref[hp, :, tm:2 * tm] = vc_ref[hp]
    key_idx = lax.broadcasted_iota(jnp.int32, (K_WINDOW, 1), 0)
    ones = jnp.ones((2 * SUBLANES, K_WINDOW), BF16)

    def where(n):
        return n // n_qb, (n % n_qb) * Q_BLOCK

    def scores(n):
        hp, r0 = where(n)
        s_ref[n % PIPE] = lax.dot_general(kcat_ref[hp, r0:r0 + K_WINDOW, :], _stack_heads(q_ref[hp, r0:r0 + Q_BLOCK, :]),
                                          (((1,), (1,)), ((), ())), preferred_element_type=F32)

    def softmax(n):
        hp, r0 = where(n)
        key_mask = jnp.where((j == 0) & (key_idx + r0 < tm), NEG_INF, 0.0).astype(F32)
        s = s_ref[n % PIPE] + bm_ref[hp] + key_mask
        e_ref[n % PIPE] = jnp.exp2(s - jnp.max(s, axis=0, keepdims=True)).astype(BF16)

    def values(n):
        hp, r0 = where(n)
        vw = jnp.concatenate([vcat_ref[hp, :, r0:r0 + K_WINDOW], ones], axis=0)
        o = _dot(vw, e_ref[n % PIPE])
        o = o[:LANES] / o[LANES:LANES + 1]
        row = lax.broadcasted_iota(jnp.int32, (LANES, Q_BLOCK), 0)
        att_ref[hp, r0:r0 + Q_BLOCK, :] = jnp.where(row < HEAD_DIM, o[:, :Q_BLOCK], o[:, Q_BLOCK:]).T

    lag = PIPE // 2
    for n in range(-lag, n_steps + lag):
        for stage, k in ((scores, n + lag), (softmax, n), (values, n - lag)):
            if 0 <= k < n_steps:
                stage(k)
    att = jnp.concatenate([att_ref[hp].astype(BF16) for hp in range(N_PAIRS)], axis=1)
    y_ref[...] = _merge_residual(x_ref, g_ref, c_ref, att, wao_ref, wout_ref)


def _attn_prompt(q, k, v, bias, x, g, c, w_att_out, w_out, layer, n_seq, tm=ATT_PAST):
    _, t, _ = q.shape
    d = D_MODEL
    nt = t // (n_seq * tm)
    cur = lambda b, j: (0, b * nt + j, 0)
    prev = lambda b, j: (0, b * nt + jnp.maximum(j - 1, 0), 0)
    cur_t = lambda b, j: (0, 0, b * nt + j)
    prev_t = lambda b, j: (0, 0, b * nt + jnp.maximum(j - 1, 0))
    row = lambda b, j: (b * nt + j, 0)
    blk = (N_PAIRS, tm, LANES)
    blk_t = (N_PAIRS, LANES, tm)
    return pl.pallas_call(
        functools.partial(_attn_prompt_kernel, tm=tm),
        grid=(n_seq, nt),
        in_specs=[pl.BlockSpec(blk, cur), pl.BlockSpec(blk, prev), pl.BlockSpec(blk, cur),
                  pl.BlockSpec(blk_t, prev_t), pl.BlockSpec(blk_t, cur_t), _layer_spec(bias.shape[1:], layer),
                  pl.BlockSpec((tm, d), row), pl.BlockSpec((tm, 2 * d), row), pl.BlockSpec((tm, d), row),
                  _layer_spec((d, d), layer), _layer_spec((d, d), layer)],
        out_specs=pl.BlockSpec((tm, d), row),
        out_shape=jax.ShapeDtypeStruct((t, d), F32),
        scratch_shapes=[pltpu.VMEM((N_PAIRS, 2 * tm, LANES), BF16), pltpu.VMEM((N_PAIRS, LANES, 2 * tm), BF16),
                        pltpu.VMEM((PIPE, K_WINDOW, 2 * Q_BLOCK), F32), pltpu.VMEM((PIPE, K_WINDOW, 2 * Q_BLOCK), BF16),
                        pltpu.VMEM((N_PAIRS, tm, LANES), F32)],
        compiler_params=_params(2),
        name="attn_prompt",
    )(q, k, k, v, v, bias, x, g, c, w_att_out, w_out)


def _attn_sample_kernel(q_ref, kc_ref, kn_ref, knf_ref, vc_ref, vn_ref, vnf_ref, bm_ref, x_ref, g_ref, c_ref,
                        wao_ref, wout_ref, y_ref, ko_ref, vo_ref, kcat_ref, vcat_ref):
    l, t = ATT_PAST, CHUNK
    pad = jnp.zeros((K_WINDOW - l - t, LANES), BF16)
    for cache_ref, new_ref, new_f32_ref, out_ref, cat_ref in ((kc_ref, kn_ref, knf_ref, ko_ref, kcat_ref),
                                                              (vc_ref, vn_ref, vnf_ref, vo_ref, vcat_ref)):
        for hp in range(N_PAIRS):
            cat_ref[hp, 0:l] = cache_ref[:, hp * LANES:(hp + 1) * LANES].astype(BF16)
            cat_ref[hp, l:l + t] = new_ref[hp]
            cat_ref[hp, l + t:K_WINDOW] = pad
        out_ref[0:l - t] = cache_ref[t:l]
        out_ref[l - t:l] = new_f32_ref[...]
    att = []
    for hp in range(N_PAIRS):
        e = _softmax_numerator(_scores(q_ref[hp], kcat_ref[hp]) + bm_ref[hp])
        att.append(_weighted_values(e, vcat_ref[hp]).astype(BF16))
    y_ref[...] = _merge_residual(x_ref, g_ref, c_ref, jnp.concatenate(att, axis=1), wao_ref, wout_ref)


def _attn_sample(q, cache_k, k_new, k_new_f32, cache_v, v_new, v_new_f32, bias, x, g, c, w_att_out, w_out, layer):
    _, t, _ = q.shape
    d = D_MODEL
    n_seq = t // CHUNK
    row = lambda b: (b, 0)
    pairs = pl.BlockSpec((N_PAIRS, CHUNK, LANES), lambda b: (0, b, 0))
    cache = pl.BlockSpec((None, None, ATT_PAST, d), lambda b: (layer, b, 0, 0))
    new_cache = pl.BlockSpec((None, ATT_PAST, d), lambda b: (b, 0, 0))
    return pl.pallas_call(
        _attn_sample_kernel,
        grid=(n_seq,),
        in_specs=[pairs, cache, pairs, pl.BlockSpec((CHUNK, d), row), cache, pairs, pl.BlockSpec((CHUNK, d), row),
                  _layer_spec(bias.shape[1:], layer),
                  pl.BlockSpec((CHUNK, d), row), pl.BlockSpec((CHUNK, 2 * d), row), pl.BlockSpec((CHUNK, d), row),
                  _layer_spec((d, d), layer), _layer_spec((d, d), layer)],
        out_specs=[pl.BlockSpec((CHUNK, d), row), new_cache, new_cache],
        out_shape=[jax.ShapeDtypeStruct((t, d), F32), jax.ShapeDtypeStruct((n_seq, ATT_PAST, d), F32),
                   jax.ShapeDtypeStruct((n_seq, ATT_PAST, d), F32)],
        scratch_shapes=[pltpu.VMEM((N_PAIRS, K_WINDOW, LANES), BF16), pltpu.VMEM((N_PAIRS, K_WINDOW, LANES), BF16)],
        compiler_params=_params(1),
        name="attn_sample",
    )(q, cache_k, k_new, k_new_f32, cache_v, v_new, v_new_f32, bias, x, g, c, w_att_out, w_out)


def _ffn_kernel(x_ref, p_ref, fh_ref, nf_ref, wup_ref, fw_ref, fb_ref, wdn_ref,
                npl_ref, wpg_ref, wpp_ref, nfin_ref, y_ref, fst_ref, up_ref, *, tm, seg_rows, tiles_per_seq, final):
    i = pl.program_id(0)
    x = x_ref[...]
    h = _rms(x, nf_ref[...]).astype(BF16)
    seqs = tm // seg_rows
    stride = FFN_HIST + seg_rows

    if seqs == 1:
        @pl.when(i % tiles_per_seq == 0)
        def _():
            up_ref[0:FFN_HIST] = fh_ref[0]

        @pl.when(i % tiles_per_seq != 0)
        def _():
            up_ref[0:FFN_HIST] = up_ref[tm:tm + FFN_HIST]
    else:
        for s in range(seqs):
            up_ref[s * stride:s * stride + FFN_HIST] = fh_ref[s]

    up = _dot(h, wup_ref[:, 0:D_FF])
    for s in range(seqs):
        up_ref[s * stride + FFN_HIST:(s + 1) * stride] = up[s * seg_rows:(s + 1) * seg_rows]
    gate = _dot(h, wup_ref[:, D_FF:2 * D_FF])
    cvs = []
    for s in range(seqs):
        fst_ref[s] = up_ref[(s + 1) * stride - FFN_HIST:(s + 1) * stride]
        cv = fb_ref[...]
        for tap in range(FFN_W):
            first = s * stride + FFN_HIST - (FFN_W - 1) + tap
            cv = cv + fw_ref[tap:tap + 1] * up_ref[first:first + seg_rows]
        cvs.append(cv)
    cv = cvs[0] if seqs == 1 else jnp.concatenate(cvs, axis=0)
    act = cv * (lax.erf(cv / jnp.sqrt(F32(2.0))) + 1.0) / 2.0 * gate
    x = x + _dot(act.astype(BF16), wdn_ref[...])

    h = _rms(x, npl_ref[...]).astype(BF16)
    x = x + jax.nn.sigmoid(_dot(h, wpg_ref[...])) * _dot(p_ref[...].astype(BF16), wpp_ref[...])
    y_ref[...] = _rms(x, nfin_ref[...]) if final else x


def _ffn(x, p, ffn_hist, hist_layer, norm_ffn, w_up, ffn_dw, ffn_b, w_down, norm_ple, w_pg, w_pp,
         norm_final, layer, seg, tm, final):
    t, d = x.shape
    n_tiles = t // tm
    seg_rows = min(seg, tm)
    seqs = tm // seg_rows
    tiles_per_seq = seg // seg_rows
    per_seq_hist = ffn_hist.shape[1] > 1
    row = lambda i: (i, 0)
    seq_blk = lambda i: (i // tiles_per_seq, 0, 0)
    kern = functools.partial(_ffn_kernel, tm=tm, seg_rows=seg_rows, tiles_per_seq=tiles_per_seq, final=final)
    return pl.pallas_call(
        kern,
        grid=(n_tiles,),
        in_specs=[pl.BlockSpec((tm, d), row), pl.BlockSpec((None, tm, PLE_DIM), lambda i: (layer, i, 0)),
                  pl.BlockSpec((None, seqs, FFN_HIST, D_FF),
                               lambda i: (hist_layer, (i // tiles_per_seq) if per_seq_hist else 0, 0, 0)),
                  _layer_spec((1, d), layer), _layer_spec((d, 2 * D_FF), layer),
                  _layer_spec((FFN_W, D_FF), layer), _layer_spec((1, D_FF), layer), _layer_spec((D_FF, d), layer),
                  _layer_spec((1, d), layer), _layer_spec((d, d), layer), _layer_spec((PLE_DIM, d), layer),
                  _const_spec((1, d))],
        out_specs=[pl.BlockSpec((tm, d), row), pl.BlockSpec((seqs, FFN_HIST, D_FF), seq_blk)],
        out_shape=[jax.ShapeDtypeStruct((t, d), F32),
                   jax.ShapeDtypeStruct((n_tiles * seqs // tiles_per_seq, FFN_HIST, D_FF), F32)],
        scratch_shapes=[pltpu.VMEM((seqs * (FFN_HIST + seg_rows), D_FF), F32)],
        compiler_params=_params(1),
        name="ffn",
    )(x, p, ffn_hist, norm_ffn, w_up, ffn_dw, ffn_b, w_down, norm_ple, w_pg, w_pp, norm_final)


def _pad_rows_front(h, rows):
    return jnp.pad(h, ((0, 0),) * (h.ndim - 2) + ((rows - h.shape[-2], 0), (0, 0)))


def kernel(x_prompt, x_sample, p_prompt, p_sample, cache_att_k, cache_att_v, state_conv, state_ffn_conv, norm_mix, w_in, conv_dw, conv_dw_b, conv_ln_g, conv_ln_b, w_conv_out, rel_table, w_att_out, b_gate, w_out, norm_ffn, w_ffn_up, ffn_dw, ffn_dw_b, w_ffn_down, norm_ple, w_ple_gate, w_ple_proj, norm_final):
    depth = w_in.shape[0]
    bp, sp, d = x_prompt.shape
    bs, ss, _ = x_sample.shape
    l_cache = cache_att_k.shape[2]
    xp = x_prompt.reshape(bp * sp, d)
    xs = x_sample.reshape(bs * ss, d)
    pp = p_prompt.reshape(depth, bp * sp, PLE_DIM)
    ps = p_sample.reshape(depth, bs * ss, PLE_DIM)
    ck = cache_att_k.reshape(depth, bs, l_cache, d)
    cv = cache_att_v.reshape(depth, bs, l_cache, d)
    rows = lambda v: v.reshape(v.shape[0], 1, v.shape[1])

    w_in_b, w_co_b, w_ao_b, w_out_b = (w.astype(BF16) for w in (w_in, w_conv_out, w_att_out, w_out))
    w_up_b, w_dn_b, w_pg_b, w_pp_b = (w.astype(BF16) for w in (w_ffn_up, w_ffn_down, w_ple_gate, w_ple_proj))
    bias_p, bias_s = _attention_bias(rel_table)
    w_vt_b = jnp.swapaxes(w_in[:, :, 4 * d:5 * d], 1, 2).astype(BF16)
    conv_hist_s = _pad_rows_front(state_conv, CONV_HIST)
    ffn_hist_s = _pad_rows_front(state_ffn_conv, FFN_HIST)
    conv_hist_p = jnp.zeros((1, 1, CONV_HIST, d), F32)
    ffn_hist_p = jnp.zeros((1, 1, FFN_HIST, D_FF), F32)
    in_args = (rows(norm_mix), w_in_b, rows(b_gate))
    conv_args = (conv_dw, rows(conv_dw_b), rows(conv_ln_g), rows(conv_ln_b), w_co_b)
    ffn_args = (rows(norm_ffn), w_up_b, ffn_dw, rows(ffn_dw_b), w_dn_b, rows(norm_ple), w_pg_b, w_pp_b,
                norm_final.reshape(1, d))
    keep = min(ATT_PAST, sp)

    outs = [[] for _ in range(8)]
    for i in range(depth):
        final = i == depth - 1
        u, q, k, v, g, kst, vst = _in_proj(xp, *in_args, layer=i, tiles_per_seq=sp // 512, w_vt=w_vt_b)
        c = _conv_module(u, conv_hist_p, 0, *conv_args, layer=i, n_seq=bp, tm=512)
        xp = _attn_prompt(q, k, v, bias_p, xp, g, c, w_ao_b, w_out_b, layer=i, n_seq=bp)
        xp, fst = _ffn(xp, pp, ffn_hist_p, 0, *ffn_args, layer=i, seg=sp, tm=512, final=final)
        outs[0].append(kst[:, 512 - keep:])
        outs[1].append(vst[:, 512 - keep:])
        outs[2].append(u.reshape(bp, sp, d)[:, sp - (CONV_W - 1):])
        outs[3].append(fst[:, FFN_HIST - (FFN_W - 1):])

        u, q, k, v, g, kst, vst = _in_proj(xs, *in_args, layer=i, tiles_per_seq=1)
        c = _conv_module(u, conv_hist_s, i, *conv_args, layer=i, n_seq=bs, tm=ss)
        xs, k_cache, v_cache = _attn_sample(q, ck, k, kst[0], cv, v, vst[0], bias_s, xs, g, c, w_ao_b, w_out_b, layer=i)
        xs, fst = _ffn(xs, ps, ffn_hist_s, i, *ffn_args, layer=i, seg=ss, tm=bs * ss, final=final)
        outs[4].append(k_cache)
        outs[5].append(v_cache)
        outs[6].append(jnp.concatenate([state_conv[i], u.reshape(bs, ss, d)], axis=1)[:, ss:])
        outs[7].append(fst[:, FFN_HIST - (FFN_W - 1):])

    heads = lambda o: jnp.stack(o).reshape(depth, -1, o[0].shape[1], N_HEADS, HEAD_DIM)
    return (xp.reshape(bp, sp, d), xs.reshape(bs, ss, d), heads(outs[0]), heads(outs[1]), jnp.stack(outs[2]),
            jnp.stack(outs[3]), heads(outs[4]), heads(outs[5]), jnp.stack(outs[6]), jnp.stack(outs[7]))
```

```python
import functools
import math

import jax
import jax.numpy as jnp
from jax import lax
from jax.experimental import pallas as pl
from jax.experimental.pallas import tpu as pltpu

D_MODEL = 1024
N_HEADS = 16
HEAD_DIM = 64
CHUNK = 64
N_PAST_CHUNKS = 8
ATT_PAST = N_PAST_CHUNKS * CHUNK
REL_CLIP = 128
CONV_W = 31
D_FF = 2816
FFN_W = 3
PLE_DIM = 256
EPS = 1e-6
NEG_INF = -1e30

LANES = 128
SUBLANES = 8
N_PAIRS = D_MODEL // LANES
Q_BLOCK = 2 * CHUNK
K_WINDOW = ATT_PAST + Q_BLOCK
ROW_TILE = ATT_PAST
SAMPLE_SEQS = 2
PIPE = 4
CONV_HIST = 32
FFN_HIST = 8
TABLE_PAD = 384
VMEM_LIMIT = 60 * 1024 * 1024
LOG2E = math.log2(math.e)
Q_SCALE = HEAD_DIM ** -0.5 * LOG2E

BF16 = jnp.bfloat16
F32 = jnp.float32


def _const_spec(shape):
    zeros = (0,) * len(shape)
    return pl.BlockSpec(shape, lambda *_: zeros, pipeline_mode=pl.Buffered(1))


def _layer_spec(shape, layer):
    zeros = (0,) * len(shape)
    return pl.BlockSpec((None, *shape), lambda *_: (layer, *zeros), pipeline_mode=pl.Buffered(1))


def _params(n_axes):
    return pltpu.CompilerParams(dimension_semantics=("arbitrary",) * n_axes, vmem_limit_bytes=VMEM_LIMIT)


def _rms(x, w):
    return x * lax.rsqrt(jnp.mean(x * x, axis=-1, keepdims=True) + EPS) * w


def _dot(a, b):
    return jnp.dot(a, b, preferred_element_type=F32)


def _bf16(w_ref):
    return w_ref[...].astype(BF16)


def _in_proj_kernel(x_ref, nw_ref, w_ref, bg_ref, *rest, tiles_per_seq, v_transposed):
    if v_transposed:
        wvt_ref, u_ref, q_ref, k_ref, v_ref, g_ref, kst_ref, vst_ref = rest
    else:
        u_ref, q_ref, k_ref, v_ref, g_ref, kst_ref, vst_ref = rest
    d = D_MODEL
    h = _rms(x_ref[...], nw_ref[...]).astype(BF16)
    glu_a = _dot(h, w_ref[:, 0:d])
    glu_b = _dot(h, w_ref[:, d:2 * d])
    u_ref[...] = glu_a * jax.nn.sigmoid(glu_b)

    def put_pairs(dst_ref, val):
        for hp in range(N_PAIRS):
            dst_ref[hp] = val[:, hp * LANES:(hp + 1) * LANES]

    put_pairs(q_ref, (_dot(h, w_ref[:, 2 * d:3 * d]) * Q_SCALE).astype(BF16))
    k = _dot(h, w_ref[:, 3 * d:4 * d])
    put_pairs(k_ref, k.astype(BF16))
    kst_ref[0] = k
    if v_transposed:
        vt = lax.dot_general(wvt_ref[...], h, (((1,), (1,)), ((), ())), preferred_element_type=F32).astype(BF16)
        for hp in range(N_PAIRS):
            v_ref[hp] = vt[hp * LANES:(hp + 1) * LANES, :]
    else:
        v = _dot(h, w_ref[:, 4 * d:5 * d])
        put_pairs(v_ref, v.astype(BF16))
        vst_ref[0] = v
    g_ref[...] = jax.nn.sigmoid(_dot(h, w_ref[:, 5 * d:7 * d]) + bg_ref[...])
    if v_transposed:
        @pl.when(pl.program_id(0) % tiles_per_seq == tiles_per_seq - 1)
        def _():
            vst_ref[0] = _dot(h, w_ref[:, 4 * d:5 * d])


def _in_proj(x, norm_w, w_in, b_gate, layer, tiles_per_seq, tm=ROW_TILE, w_vt=None):
    t, d = x.shape
    n_tiles = t // tm
    n_seq = n_tiles // tiles_per_seq
    row = lambda i: (i, 0)
    st = lambda i: (i // tiles_per_seq, 0, 0)
    pairs = pl.BlockSpec((N_PAIRS, tm, LANES), lambda i: (0, i, 0))
    pairs_shape = jax.ShapeDtypeStruct((N_PAIRS, t, LANES), BF16)
    vt = w_vt is not None
    v_spec = pl.BlockSpec((N_PAIRS, LANES, tm), lambda i: (0, 0, i)) if vt else pairs
    v_shape = jax.ShapeDtypeStruct((N_PAIRS, LANES, t), BF16) if vt else pairs_shape
    return pl.pallas_call(
        functools.partial(_in_proj_kernel, tiles_per_seq=tiles_per_seq, v_transposed=vt),
        grid=(n_tiles,),
        in_specs=[pl.BlockSpec((tm, d), row), _layer_spec((1, d), layer), _layer_spec((d, 7 * d), layer),
                  _layer_spec((1, 2 * d), layer)] + ([_layer_spec((d, d), layer)] if vt else []),
        out_specs=[pl.BlockSpec((tm, d), row), pairs, pairs, v_spec, pl.BlockSpec((tm, 2 * d), row),
                   pl.BlockSpec((1, tm, d), st), pl.BlockSpec((1, tm, d), st)],
        out_shape=[jax.ShapeDtypeStruct((t, d), F32), pairs_shape, pairs_shape, v_shape,
                   jax.ShapeDtypeStruct((t, 2 * d), F32),
                   jax.ShapeDtypeStruct((n_seq, tm, d), F32), jax.ShapeDtypeStruct((n_seq, tm, d), F32)],
        compiler_params=_params(1),
        name="in_proj",
    )(x, norm_w, w_in, b_gate, *([w_vt] if vt else []))


def _conv_kernel(u_ref, hist_ref, w_ref, b_ref, lg_ref, lb_ref, wo_ref, c_ref, xe_ref, xs_ref, y_ref, *, tm, rows):
    j = pl.program_id(1)

    @pl.when(j == 0)
    def _():
        for cb in range(D_MODEL // LANES):
            xe_ref[cb, 0:CONV_HIST] = hist_ref[:, cb * LANES:(cb + 1) * LANES]

    @pl.when(j > 0)
    def _():
        for cb in range(D_MODEL // LANES):
            xe_ref[cb, 0:CONV_HIST] = xe_ref[cb, tm:tm + CONV_HIST]

    n_shift = CONV_HIST + tm - SUBLANES
    first = CONV_HIST - (CONV_W - 1)
    for cb in range(D_MODEL // LANES):
        cs = slice(cb * LANES, (cb + 1) * LANES)
        xe_ref[cb, CONV_HIST:CONV_HIST + tm] = u_ref[:, cs]
        for r in range(1, SUBLANES):
            xs_ref[cb, r - 1] = xe_ref[cb, r:r + n_shift]

        for r0 in range(0, tm, rows):
            acc = jnp.broadcast_to(b_ref[:, cs], (rows, LANES))
            for tap in range(CONV_W):
                shift = (first + tap) % SUBLANES
                base = first + tap - shift
                if shift == 0:
                    win = xe_ref[cb, r0 + base:r0 + base + rows, :]
                else:
                    win = xs_ref[cb, shift - 1, r0 + base:r0 + base + rows, :]
                acc = acc + w_ref[tap:tap + 1, cs] * win
            y_ref[r0:r0 + rows, cs] = acc

    y = y_ref[...]
    mu = jnp.mean(y, axis=-1, keepdims=True)
    yc = y - mu
    var = jnp.mean(yc * yc, axis=-1, keepdims=True)
    z = yc * lax.rsqrt(var + EPS) * lg_ref[...] + lb_ref[...]
    c_ref[...] = _dot(jax.nn.silu(z).astype(BF16), _bf16(wo_ref))


def _conv_module(u, hist, hist_layer, conv_dw, conv_b, ln_g, ln_b, w_conv_out, layer, n_seq, tm, rows=64):
    t, d = u.shape
    nt = t // (n_seq * tm)
    per_seq_hist = hist.shape[1] > 1
    kern = functools.partial(_conv_kernel, tm=tm, rows=rows)
    row = lambda b, j: (b * nt + j, 0)
    return pl.pallas_call(
        kern,
        grid=(n_seq, nt),
        in_specs=[pl.BlockSpec((tm, d), row),
                  pl.BlockSpec((None, None, CONV_HIST, d), lambda b, j: (hist_layer, b if per_seq_hist else 0, 0, 0)),
                  _layer_spec((CONV_W, d), layer), _layer_spec((1, d), layer), _layer_spec((1, d), layer),
                  _layer_spec((1, d), layer), _layer_spec((d, d), layer)],
        out_specs=pl.BlockSpec((tm, d), row),
        out_shape=jax.ShapeDtypeStruct((t, d), F32),
        scratch_shapes=[pltpu.VMEM((d // LANES, CONV_HIST + tm, LANES), F32),
                        pltpu.VMEM((d // LANES, SUBLANES - 1, CONV_HIST + tm - SUBLANES, LANES), F32),
                        pltpu.VMEM((tm, d), F32)],
        compiler_params=_params(2),
        name="conv_module",
    )(u, hist, conv_dw, conv_b, ln_g, ln_b, w_conv_out)


def _bias_kernel(t_ref, bt_ref, bs_ref):
    t = t_ref[...]
    hi = t.astype(BF16)
    rest = t - hi.astype(F32)
    mid = rest.astype(BF16)
    lo = (rest - mid.astype(F32)).astype(BF16)
    idx = lax.broadcasted_iota(jnp.int32, (TABLE_PAD, K_WINDOW), 0)
    col = lax.broadcasted_iota(jnp.int32, (TABLE_PAD, K_WINDOW), 1)
    sel = jnp.where(col < ATT_PAST + CHUNK, jnp.clip(ATT_PAST - col, -REL_CLIP, REL_CLIP) + REL_CLIP, 2 * REL_CLIP)
    onehot = jnp.where(idx == sel, 1.0, 0.0).astype(BF16)
    row0 = (_dot(hi, onehot) + _dot(mid, onehot) + _dot(lo, onehot)) * LOG2E
    r = lax.broadcasted_iota(jnp.int32, (Q_BLOCK, K_WINDOW), 0)
    c = lax.broadcasted_iota(jnp.int32, (Q_BLOCK, K_WINDOW), 1)
    band = ((r < CHUNK) & (c < ATT_PAST + CHUNK)) | ((r >= CHUNK) & (c >= CHUNK))
    for h in range(N_HEADS):
        x = pltpu.roll(jnp.broadcast_to(row0[h:h + 1], (Q_BLOCK, K_WINDOW)), 0, 1, stride=1, stride_axis=0)
        x = jnp.where(band, x, NEG_INF)
        bt_ref[h // 2, :, (h % 2) * Q_BLOCK:(h % 2 + 1) * Q_BLOCK] = x.T
        bs_ref[h] = x[:CHUNK]


def _attention_bias(rel_table):
    assert ATT_PAST >= REL_CLIP and rel_table.shape[2] == 2 * REL_CLIP + 1
    depth = rel_table.shape[0]
    table = jnp.pad(rel_table, ((0, 0), (0, 0), (0, TABLE_PAD - rel_table.shape[2])))
    bt, bs = pl.pallas_call(
        _bias_kernel,
        grid=(depth,),
        in_specs=[pl.BlockSpec((None, N_HEADS, TABLE_PAD), lambda l: (l, 0, 0))],
        out_specs=[pl.BlockSpec((None, N_PAIRS, K_WINDOW, 2 * Q_BLOCK), lambda l: (l, 0, 0, 0)),
                   pl.BlockSpec((None, N_HEADS, CHUNK, K_WINDOW), lambda l: (l, 0, 0, 0))],
        out_shape=[jax.ShapeDtypeStruct((depth, N_PAIRS, K_WINDOW, 2 * Q_BLOCK), F32),
                   jax.ShapeDtypeStruct((depth, N_HEADS, CHUNK, K_WINDOW), F32)],
        compiler_params=_params(1),
        name="rel_bias",
    )(table)
    return bt, bs.reshape(depth, N_PAIRS, 2 * CHUNK, K_WINDOW)


def _stack_heads(q_blk):
    lane = lax.broadcasted_iota(jnp.int32, q_blk.shape, 1)
    zero = jnp.zeros_like(q_blk)
    return jnp.concatenate([jnp.where(lane < HEAD_DIM, q_blk, zero), jnp.where(lane >= HEAD_DIM, q_blk, zero)], axis=0)


def _scores(q_blk, k_win):
    return lax.dot_general(_stack_heads(q_blk), k_win, (((1,), (1,)), ((), ())), preferred_element_type=F32)


def _softmax_numerator(s):
    return jnp.exp2(s - jnp.max(s, axis=-1, keepdims=True)).astype(BF16)


def _weighted_values(e, v_win):
    qb = e.shape[0] // 2
    o = _dot(e, jnp.concatenate([v_win, jnp.ones_like(v_win)], axis=1))
    o = o[:, :LANES] / o[:, LANES:]
    lane = lax.broadcasted_iota(jnp.int32, (qb, LANES), 1)
    return jnp.where(lane < HEAD_DIM, o[:qb], o[qb:])


def _merge_residual(x_ref, g_ref, c_ref, att, wao_ref, wout_ref):
    d = D_MODEL
    merged = g_ref[:, 0:d] * c_ref[...] + g_ref[:, d:2 * d] * _dot(att, _bf16(wao_ref))
    return x_ref[...] + _dot(merged.astype(BF16), _bf16(wout_ref))


def _attn_prompt_kernel(q_ref, kp_ref, kc_ref, vp_ref, vc_ref, bm_ref, x_ref, g_ref, c_ref, wao_ref, wout_ref, y_ref,
                        kcat_ref, vcat_ref, s_ref, e_ref, att_ref, *, tm):
    j = pl.program_id(1)
    n_qb = tm // Q_BLOCK
    n_steps = N_PAIRS * n_qb
    for hp in range(N_PAIRS):
        kcat_ref[hp, 0:tm] = kp_ref[hp]
        kcat_ref[hp, tm:2 * tm] = kc_ref[hp]
        vcat_ref[hp, :, 0:tm] = vp_ref[hp]
        vcat_ref[hp, :, tm:2 * tm] = vc_ref[hp]
    key_idx = lax.broadcasted_iota(jnp.int32, (K_WINDOW, 1), 0)
    ones = jnp.ones((2 * SUBLANES, K_WINDOW), BF16)

    def where(n):
        return n // n_qb, (n % n_qb) * Q_BLOCK

    def scores(n):
        hp, r0 = where(n)
        s_ref[n % PIPE] = lax.dot_general(kcat_ref[hp, r0:r0 + K_WINDOW, :], _stack_heads(q_ref[hp, r0:r0 + Q_BLOCK, :]),
                                          (((1,), (1,)), ((), ())), preferred_element_type=F32)

    def softmax(n):
        hp, r0 = where(n)
        key_mask = jnp.where((j == 0) & (key_idx + r0 < tm), NEG_INF, 0.0).astype(F32)
        s = s_ref[n % PIPE] + bm_ref[hp] + key_mask
        e_ref[n % PIPE] = jnp.exp2(s - jnp.max(s, axis=0, keepdims=True)).astype(BF16)

    def values(n):
        hp, r0 = where(n)
        vw = jnp.concatenate([vcat_ref[hp, :, r0:r0 + K_WINDOW], ones], axis=0)
        o = _dot(vw, e_ref[n % PIPE])
        o = o[:LANES] / o[LANES:LANES + 1]
        row = lax.broadcasted_iota(jnp.int32, (LANES, Q_BLOCK), 0)
        att_ref[hp, r0:r0 + Q_BLOCK, :] = jnp.where(row < HEAD_DIM, o[:, :Q_BLOCK], o[:, Q_BLOCK:]).T

    lag = PIPE // 2
    for n in range(-lag, n_steps + lag):
        for stage, k in ((scores, n + lag), (softmax, n), (values, n - lag)):
            if 0 <= k < n_steps:
                stage(k)
    att = jnp.concatenate([att_ref[hp].astype(BF16) for hp in range(N_PAIRS)], axis=1)
    y_ref[...] = _merge_residual(x_ref, g_ref, c_ref, att, wao_ref, wout_ref)


def _attn_prompt(q, k, v, bias, x, g, c, w_att_out, w_out, layer, n_seq, tm=ATT_PAST):
    _, t, _ = q.shape
    d = D_MODEL
    nt = t // (n_seq * tm)
    cur = lambda b, j: (0, b * nt + j, 0)
    prev = lambda b, j: (0, b * nt + jnp.maximum(j - 1, 0), 0)
    cur_t = lambda b, j: (0, 0, b * nt + j)
    prev_t = lambda b, j: (0, 0, b * nt + jnp.maximum(j - 1, 0))
    row = lambda b, j: (b * nt + j, 0)
    blk = (N_PAIRS, tm, LANES)
    blk_t = (N_PAIRS, LANES, tm)
    return pl.pallas_call(
        functools.partial(_attn_prompt_kernel, tm=tm),
        grid=(n_seq, nt),
        in_specs=[pl.BlockSpec(blk, cur), pl.BlockSpec(blk, prev), pl.BlockSpec(blk, cur),
                  pl.BlockSpec(blk_t, prev_t), pl.BlockSpec(blk_t, cur_t), _layer_spec(bias.shape[1:], layer),
                  pl.BlockSpec((tm, d), row), pl.BlockSpec((tm, 2 * d), row), pl.BlockSpec((tm, d), row),
                  _layer_spec((d, d), layer), _layer_spec((d, d), layer)],
        out_specs=pl.BlockSpec((tm, d), row),
        out_shape=jax.ShapeDtypeStruct((t, d), F32),
        scratch_shapes=[pltpu.VMEM((N_PAIRS, 2 * tm, LANES), BF16), pltpu.VMEM((N_PAIRS, LANES, 2 * tm), BF16),
                        pltpu.VMEM((PIPE, K_WINDOW, 2 * Q_BLOCK), F32), pltpu.VMEM((PIPE, K_WINDOW, 2 * Q_BLOCK), BF16),
                        pltpu.VMEM((N_PAIRS, tm, LANES), F32)],
        compiler_params=_params(2),
        name="attn_prompt",
    )(q, k, k, v, v, bias, x, g, c, w_att_out, w_out)


def _attn_sample_kernel(q_ref, kc_ref, kn_ref, knf_ref, vc_ref, vn_ref, vnf_ref, bm_ref, x_ref, g_ref, c_ref,
                        wao_ref, wout_ref, y_ref, ko_ref, vo_ref, kcat_ref, vcat_ref, *, seqs):
    l, t = ATT_PAST, CHUNK
    pad = jnp.zeros((K_WINDOW - l - t, LANES), BF16)
    att_rows = []
    for s in range(seqs):
        rs = slice(s * t, (s + 1) * t)
        for cache_ref, new_ref, new_f32_ref, out_ref, cat_ref in ((kc_ref, kn_ref, knf_ref, ko_ref, kcat_ref),
                                                                  (vc_ref, vn_ref, vnf_ref, vo_ref, vcat_ref)):
            for hp in range(N_PAIRS):
                cat_ref[hp, 0:l] = cache_ref[s, :, hp * LANES:(hp + 1) * LANES].astype(BF16)
                cat_ref[hp, l:l + t] = new_ref[hp, rs, :]
                cat_ref[hp, l + t:K_WINDOW] = pad
            out_ref[s, 0:l - t] = cache_ref[s, t:l]
            out_ref[s, l - t:l] = new_f32_ref[rs, :]
        att = []
        for hp in range(N_PAIRS):
            e = _softmax_numerator(_scores(q_ref[hp, rs, :], kcat_ref[hp]) + bm_ref[hp])
            att.append(_weighted_values(e, vcat_ref[hp]).astype(BF16))
        att_rows.append(jnp.concatenate(att, axis=1))
    y_ref[...] = _merge_residual(x_ref, g_ref, c_ref, jnp.concatenate(att_rows, axis=0), wao_ref, wout_ref)


def _attn_sample(q, cache_k, k_new, k_new_f32, cache_v, v_new, v_new_f32, bias, x, g, c, w_att_out, w_out, layer):
    _, t, _ = q.shape
    d = D_MODEL
    n_seq = t // CHUNK
    seqs = SAMPLE_SEQS
    assert n_seq % seqs == 0
    rows = seqs * CHUNK
    row = lambda b: (b, 0)
    pairs = pl.BlockSpec((N_PAIRS, rows, LANES), lambda b: (0, b, 0))
    cache = pl.BlockSpec((None, seqs, ATT_PAST, d), lambda b: (layer, b, 0, 0))
    new_cache = pl.BlockSpec((seqs, ATT_PAST, d), lambda b: (b, 0, 0))
    return pl.pallas_call(
        functools.partial(_attn_sample_kernel, seqs=seqs),
        grid=(n_seq // seqs,),
        in_specs=[pairs, cache, pairs, pl.BlockSpec((rows, d), row), cache, pairs, pl.BlockSpec((rows, d), row),
                  _layer_spec(bias.shape[1:], layer),
                  pl.BlockSpec((rows, d), row), pl.BlockSpec((rows, 2 * d), row), pl.BlockSpec((rows, d), row),
                  _layer_spec((d, d), layer), _layer_spec((d, d), layer)],
        out_specs=[pl.BlockSpec((rows, d), row), new_cache, new_cache],
        out_shape=[jax.ShapeDtypeStruct((t, d), F32), jax.ShapeDtypeStruct((n_seq, ATT_PAST, d), F32),
                   jax.ShapeDtypeStruct((n_seq, ATT_PAST, d), F32)],
        scratch_shapes=[pltpu.VMEM((N_PAIRS, K_WINDOW, LANES), BF16), pltpu.VMEM((N_PAIRS, K_WINDOW, LANES), BF16)],
        compiler_params=_params(1),
        name="attn_sample",
    )(q, cache_k, k_new, k_new_f32, cache_v, v_new, v_new_f32, bias, x, g, c, w_att_out, w_out)


def _ffn_kernel(x_ref, p_ref, fh_ref, nf_ref, wup_ref, fw_ref, fb_ref, wdn_ref,
                npl_ref, wpg_ref, wpp_ref, nfin_ref, y_ref, fst_ref, up_ref, *, tm, seg_rows, tiles_per_seq, final):
    i = pl.program_id(0)
    seqs = tm // seg_rows
    stride = FFN_HIST + seg_rows

    if seqs == 1:
        @pl.when(i % tiles_per_seq == 0)
        def _():
            up_ref[0:FFN_HIST] = fh_ref[0]

        @pl.when(i % tiles_per_seq != 0)
        def _():
            up_ref[0:FFN_HIST] = up_ref[tm:tm + FFN_HIST]
    else:
        for s in range(seqs):
            up_ref[s * stride:s * stride + FFN_HIST] = fh_ref[s]

    x = x_ref[...]
    h = _rms(x, nf_ref[...]).astype(BF16)
    up = _dot(h, wup_ref[:, 0:D_FF])
    for s in range(seqs):
        up_ref[s * stride + FFN_HIST:(s + 1) * stride] = up[s * seg_rows:(s + 1) * seg_rows]
    gate = _dot(h, wup_ref[:, D_FF:2 * D_FF])
    cvs = []
    for s in range(seqs):
        fst_ref[s] = up_ref[(s + 1) * stride - FFN_HIST:(s + 1) * stride]
        cv = fb_ref[...]
        for tap in range(FFN_W):
            first = s * stride + FFN_HIST - (FFN_W - 1) + tap
            cv = cv + fw_ref[tap:tap + 1] * up_ref[first:first + seg_rows]
        cvs.append(cv)
    cv = cvs[0] if seqs == 1 else jnp.concatenate(cvs, axis=0)
    act = cv * (lax.erf(cv / jnp.sqrt(F32(2.0))) + 1.0) / 2.0 * gate
    x = x + _dot(act.astype(BF16), wdn_ref[...])

    h = _rms(x, npl_ref[...]).astype(BF16)
    x = x + jax.nn.sigmoid(_dot(h, _bf16(wpg_ref))) * _dot(p_ref[...].astype(BF16), _bf16(wpp_ref))
    y_ref[...] = _rms(x, nfin_ref[...]) if final else x


def _ffn(x, p, ffn_hist, hist_layer, norm_ffn, w_up, ffn_dw, ffn_b, w_down, norm_ple, w_pg, w_pp,
         norm_final, layer, seg, tm, final):
    t, d = x.shape
    n_tiles = t // tm
    seg_rows = min(seg, tm)
    seqs = tm // seg_rows
    tiles_per_seq = seg // seg_rows
    per_seq_hist = ffn_hist.shape[1] > 1
    row = lambda i: (i, 0)
    seq_blk = lambda i: (i // tiles_per_seq, 0, 0)
    kern = functools.partial(_ffn_kernel, tm=tm, seg_rows=seg_rows, tiles_per_seq=tiles_per_seq, final=final)
    return pl.pallas_call(
        kern,
        grid=(n_tiles,),
        in_specs=[pl.BlockSpec((tm, d), row), pl.BlockSpec((None, tm, PLE_DIM), lambda i: (layer, i, 0)),
                  pl.BlockSpec((None, seqs, FFN_HIST, D_FF),
                               lambda i: (hist_layer, (i // tiles_per_seq) if per_seq_hist else 0, 0, 0)),
                  _layer_spec((1, d), layer), _layer_spec((d, 2 * D_FF), layer),
                  _layer_spec((FFN_W, D_FF), layer), _layer_spec((1, D_FF), layer), _layer_spec((D_FF, d), layer),
                  _layer_spec((1, d), layer), _layer_spec((d, d), layer), _layer_spec((PLE_DIM, d), layer),
                  _const_spec((1, d))],
        out_specs=[pl.BlockSpec((tm, d), row), pl.BlockSpec((seqs, FFN_HIST, D_FF), seq_blk)],
        out_shape=[jax.ShapeDtypeStruct((t, d), F32),
                   jax.ShapeDtypeStruct((n_tiles * seqs // tiles_per_seq, FFN_HIST, D_FF), F32)],
        scratch_shapes=[pltpu.VMEM((seqs * (FFN_HIST + seg_rows), D_FF), F32)],
        compiler_params=_params(1),
        name="ffn",
    )(x, p, ffn_hist, norm_ffn, w_up, ffn_dw, ffn_b, w_down, norm_ple, w_pg, w_pp, norm_final)


def _pad_rows_front(h, rows):
    return jnp.pad(h, ((0, 0),) * (h.ndim - 2) + ((rows - h.shape[-2], 0), (0, 0)))


def kernel(x_prompt, x_sample, p_prompt, p_sample, cache_att_k, cache_att_v, state_conv, state_ffn_conv, norm_mix, w_in, conv_dw, conv_dw_b, conv_ln_g, conv_ln_b, w_conv_out, rel_table, w_att_out, b_gate, w_out, norm_ffn, w_ffn_up, ffn_dw, ffn_dw_b, w_ffn_down, norm_ple, w_ple_gate, w_ple_proj, norm_final):
    depth = w_in.shape[0]
    bp, sp, d = x_prompt.shape
    bs, ss, _ = x_sample.shape
    l_cache = cache_att_k.shape[2]
    xp = x_prompt.reshape(bp * sp, d)
    xs = x_sample.reshape(bs * ss, d)
    pp = p_prompt.reshape(depth, bp * sp, PLE_DIM)
    ps = p_sample.reshape(depth, bs * ss, PLE_DIM)
    ck = cache_att_k.reshape(depth, bs, l_cache, d)
    cv = cache_att_v.reshape(depth, bs, l_cache, d)
    rows = lambda v: v.reshape(v.shape[0], 1, v.shape[1])

    w_in_b, w_up_b, w_dn_b = (w.astype(BF16) for w in (w_in, w_ffn_up, w_ffn_down))
    bias_p, bias_s = _attention_bias(rel_table)
    w_vt_b = jnp.swapaxes(w_in[:, :, 4 * d:5 * d], 1, 2).astype(BF16)
    conv_hist_s = _pad_rows_front(state_conv, CONV_HIST)
    ffn_hist_s = _pad_rows_front(state_ffn_conv, FFN_HIST)
    conv_hist_p = jnp.zeros((1, 1, CONV_HIST, d), F32)
    ffn_hist_p = jnp.zeros((1, 1, FFN_HIST, D_FF), F32)
    in_args = (rows(norm_mix), w_in_b, rows(b_gate))
    conv_args = (conv_dw, rows(conv_dw_b), rows(conv_ln_g), rows(conv_ln_b), w_conv_out)
    ffn_args = (rows(norm_ffn), w_up_b, ffn_dw, rows(ffn_dw_b), w_dn_b, rows(norm_ple), w_ple_gate, w_ple_proj,
                norm_final.reshape(1, d))
    keep = min(ATT_PAST, sp)

    outs = [[] for _ in range(8)]
    for i in range(depth):
        final = i == depth - 1
        u, q, k, v, g, kst, vst = _in_proj(xp, *in_args, layer=i, tiles_per_seq=sp // ROW_TILE, w_vt=w_vt_b)
        c = _conv_module(u, conv_hist_p, 0, *conv_args, layer=i, n_seq=bp, tm=ROW_TILE)
        xp = _attn_prompt(q, k, v, bias_p, xp, g, c, w_att_out, w_out, layer=i, n_seq=bp)
        xp, fst = _ffn(xp, pp, ffn_hist_p, 0, *ffn_args, layer=i, seg=sp, tm=ROW_TILE, final=final)
        outs[0].append(kst[:, ROW_TILE - keep:])
        outs[1].append(vst[:, ROW_TILE - keep:])
        outs[2].append(u.reshape(bp, sp, d)[:, sp - (CONV_W - 1):])
        outs[3].append(fst[:, FFN_HIST - (FFN_W - 1):])

        u, q, k, v, g, kst, vst = _in_proj(xs, *in_args, layer=i, tiles_per_seq=1)
        c = _conv_module(u, conv_hist_s, i, *conv_args, layer=i, n_seq=bs, tm=ss)
        xs, k_cache, v_cache = _attn_sample(q, ck, k, kst[0], cv, v, vst[0], bias_s, xs, g, c, w_att_out, w_out, layer=i)
        xs, fst = _ffn(xs, ps, ffn_hist_s, i, *ffn_args, layer=i, seg=ss, tm=bs * ss, final=final)
        outs[4].append(k_cache)
        outs[5].append(v_cache)
        outs[6].append(jnp.concatenate([state_conv[i], u.reshape(bs, ss, d)], axis=1)[:, ss:])
        outs[7].append(fst[:, FFN_HIST - (FFN_W - 1):])

    heads = lambda o: jnp.stack(o).reshape(depth, -1, o[0].shape[1], N_HEADS, HEAD_DIM)
    return (xp.reshape(bp, sp, d), xs.reshape(bs, ss, d), heads(outs[0]), heads(outs[1]), jnp.stack(outs[2]),
            jnp.stack(outs[3]), heads(outs[4]), heads(outs[5]), jnp.stack(outs[6]), jnp.stack(outs[7]))
```

```python
import functools
import math

import jax
import jax.numpy as jnp
from jax import lax
from jax.experimental import pallas as pl
from jax.experimental.pallas import tpu as pltpu

D_MODEL = 1024
N_HEADS = 16
HEAD_DIM = 64
CHUNK = 64
N_PAST_CHUNKS = 8
ATT_PAST = N_PAST_CHUNKS * CHUNK
REL_CLIP = 128
CONV_W = 31
D_FF = 2816
FFN_W = 3
PLE_DIM = 256
EPS = 1e-6
NEG_INF = -1e30

LANES = 128
SUBLANES = 8
N_PAIRS = D_MODEL // LANES
Q_BLOCK = 2 * CHUNK
K_WINDOW = ATT_PAST + Q_BLOCK
ROW_TILE = ATT_PAST
SAMPLE_SEQS = 2
PIPE = 4
CONV_HIST = 32
FFN_HIST = 8
TABLE_PAD = 384
VMEM_LIMIT = 60 * 1024 * 1024
LOG2E = math.log2(math.e)
Q_SCALE = HEAD_DIM ** -0.5 * LOG2E

BF16 = jnp.bfloat16
F32 = jnp.float32


def _const_spec(shape):
    zeros = (0,) * len(shape)
    return pl.BlockSpec(shape, lambda *_: zeros, pipeline_mode=pl.Buffered(1))


def _layer_spec(shape, layer):
    zeros = (0,) * len(shape)
    return pl.BlockSpec((None, *shape), lambda *_: (layer, *zeros), pipeline_mode=pl.Buffered(1))


def _params(n_axes):
    return pltpu.CompilerParams(dimension_semantics=("arbitrary",) * n_axes, vmem_limit_bytes=VMEM_LIMIT)


def _rms(x, w):
    return x * lax.rsqrt(jnp.mean(x * x, axis=-1, keepdims=True) + EPS) * w


def _dot(a, b):
    return jnp.dot(a, b, preferred_element_type=F32)


def _bf16(w_ref):
    return w_ref[...].astype(BF16)


def _in_proj_kernel(x_ref, nw_ref, w_ref, bg_ref, *rest, tiles_per_seq, v_transposed):
    if v_transposed:
        wvt_ref, u_ref, q_ref, k_ref, v_ref, g_ref, kst_ref, vst_ref = rest
    else:
        u_ref, q_ref, k_ref, v_ref, g_ref, kst_ref, vst_ref = rest
    d = D_MODEL
    h = _rms(x_ref[...], nw_ref[...]).astype(BF16)
    glu_a = _dot(h, w_ref[:, 0:d])
    glu_b = _dot(h, w_ref[:, d:2 * d])
    u_ref[...] = glu_a * jax.nn.sigmoid(glu_b)

    def put_pairs(dst_ref, val):
        for hp in range(N_PAIRS):
            dst_ref[hp] = val[:, hp * LANES:(hp + 1) * LANES]

    put_pairs(q_ref, (_dot(h, w_ref[:, 2 * d:3 * d]) * Q_SCALE).astype(BF16))
    k = _dot(h, w_ref[:, 3 * d:4 * d])
    put_pairs(k_ref, k.astype(BF16))
    kst_ref[0] = k
    if v_transposed:
        vt = lax.dot_general(wvt_ref[...], h, (((1,), (1,)), ((), ())), preferred_element_type=F32).astype(BF16)
        for hp in range(N_PAIRS):
            v_ref[hp] = vt[hp * LANES:(hp + 1) * LANES, :]
    else:
        v = _dot(h, w_ref[:, 4 * d:5 * d])
        put_pairs(v_ref, v.astype(BF16))
        vst_ref[0] = v
    g_ref[...] = jax.nn.sigmoid(_dot(h, w_ref[:, 5 * d:7 * d]) + bg_ref[...])
    if v_transposed:
        @pl.when(pl.program_id(0) % tiles_per_seq == tiles_per_seq - 1)
        def _():
            vst_ref[0] = _dot(h, w_ref[:, 4 * d:5 * d])


def _in_proj(x, norm_w, w_in, b_gate, layer, tiles_per_seq, tm=ROW_TILE, w_vt=None):
    t, d = x.shape
    n_tiles = t // tm
    n_seq = n_tiles // tiles_per_seq
    row = lambda i: (i, 0)
    st = lambda i: (i // tiles_per_seq, 0, 0)
    pairs = pl.BlockSpec((N_PAIRS, tm, LANES), lambda i: (0, i, 0))
    pairs_shape = jax.ShapeDtypeStruct((N_PAIRS, t, LANES), BF16)
    vt = w_vt is not None
    v_spec = pl.BlockSpec((N_PAIRS, LANES, tm), lambda i: (0, 0, i)) if vt else pairs
    v_shape = jax.ShapeDtypeStruct((N_PAIRS, LANES, t), BF16) if vt else pairs_shape
    return pl.pallas_call(
        functools.partial(_in_proj_kernel, tiles_per_seq=tiles_per_seq, v_transposed=vt),
        grid=(n_tiles,),
        in_specs=[pl.BlockSpec((tm, d), row), _layer_spec((1, d), layer), _layer_spec((d, 7 * d), layer),
                  _layer_spec((1, 2 * d), layer)] + ([_layer_spec((d, d), layer)] if vt else []),
        out_specs=[pl.BlockSpec((tm, d), row), pairs, pairs, v_spec, pl.BlockSpec((tm, 2 * d), row),
                   pl.BlockSpec((1, tm, d), st), pl.BlockSpec((1, tm, d), st)],
        out_shape=[jax.ShapeDtypeStruct((t, d), F32), pairs_shape, pairs_shape, v_shape,
                   jax.ShapeDtypeStruct((t, 2 * d), F32),
                   jax.ShapeDtypeStruct((n_seq, tm, d), F32), jax.ShapeDtypeStruct((n_seq, tm, d), F32)],
        compiler_params=_params(1),
        name="in_proj",
    )(x, norm_w, w_in, b_gate, *([w_vt] if vt else []))


def _conv_kernel(u_ref, hist_ref, w_ref, b_ref, lg_ref, lb_ref, wo_ref, c_ref, xe_ref, xs_ref, y_ref, *, tm, rows):
    j = pl.program_id(1)

    @pl.when(j == 0)
    def _():
        for cb in range(D_MODEL // LANES):
            xe_ref[cb, 0:CONV_HIST] = hist_ref[:, cb * LANES:(cb + 1) * LANES]

    @pl.when(j > 0)
    def _():
        for cb in range(D_MODEL // LANES):
            xe_ref[cb, 0:CONV_HIST] = xe_ref[cb, tm:tm + CONV_HIST]

    n_shift = CONV_HIST + tm - SUBLANES
    first = CONV_HIST - (CONV_W - 1)
    for cb in range(D_MODEL // LANES):
        cs = slice(cb * LANES, (cb + 1) * LANES)
        xe_ref[cb, CONV_HIST:CONV_HIST + tm] = u_ref[:, cs]
        for r in range(1, SUBLANES):
            xs_ref[cb, r - 1] = xe_ref[cb, r:r + n_shift]

        for r0 in range(0, tm, rows):
            acc = jnp.broadcast_to(b_ref[:, cs], (rows, LANES))
            for tap in range(CONV_W):
                shift = (first + tap) % SUBLANES
                base = first + tap - shift
                if shift == 0:
                    win = xe_ref[cb, r0 + base:r0 + base + rows, :]
                else:
                    win = xs_ref[cb, shift - 1, r0 + base:r0 + base + rows, :]
                acc = acc + w_ref[tap:tap + 1, cs] * win
            y_ref[r0:r0 + rows, cs] = acc

    y = y_ref[...]
    mu = jnp.mean(y, axis=-1, keepdims=True)
    yc = y - mu
    var = jnp.mean(yc * yc, axis=-1, keepdims=True)
    z = yc * lax.rsqrt(var + EPS) * lg_ref[...] + lb_ref[...]
    c_ref[...] = _dot(jax.nn.silu(z).astype(BF16), _bf16(wo_ref))


def _conv_module(u, hist, hist_layer, conv_dw, conv_b, ln_g, ln_b, w_conv_out, layer, n_seq, tm, rows=64):
    t, d = u.shape
    nt = t // (n_seq * tm)
    per_seq_hist = hist.shape[1] > 1
    kern = functools.partial(_conv_kernel, tm=tm, rows=rows)
    row = lambda b, j: (b * nt + j, 0)
    return pl.pallas_call(
        kern,
        grid=(n_seq, nt),
        in_specs=[pl.BlockSpec((tm, d), row),
                  pl.BlockSpec((None, None, CONV_HIST, d), lambda b, j: (hist_layer, b if per_seq_hist else 0, 0, 0)),
                  _layer_spec((CONV_W, d), layer), _layer_spec((1, d), layer), _layer_spec((1, d), layer),
                  _layer_spec((1, d), layer), _layer_spec((d, d), layer)],
        out_specs=pl.BlockSpec((tm, d), row),
        out_shape=jax.ShapeDtypeStruct((t, d), F32),
        scratch_shapes=[pltpu.VMEM((d // LANES, CONV_HIST + tm, LANES), F32),
                        pltpu.VMEM((d // LANES, SUBLANES - 1, CONV_HIST + tm - SUBLANES, LANES), F32),
                        pltpu.VMEM((tm, d), F32)],
        compiler_params=_params(2),
        name="conv_module",
    )(u, hist, conv_dw, conv_b, ln_g, ln_b, w_conv_out)


def _bias_kernel(t_ref, bt_ref, bs_ref):
    t = t_ref[...]
    hi = t.astype(BF16)
    rest = t - hi.astype(F32)
    mid = rest.astype(BF16)
    lo = (rest - mid.astype(F32)).astype(BF16)
    idx = lax.broadcasted_iota(jnp.int32, (TABLE_PAD, K_WINDOW), 0)
    col = lax.broadcasted_iota(jnp.int32, (TABLE_PAD, K_WINDOW), 1)
    sel = jnp.where(col < ATT_PAST + CHUNK, jnp.clip(ATT_PAST - col, -REL_CLIP, REL_CLIP) + REL_CLIP, 2 * REL_CLIP)
    onehot = jnp.where(idx == sel, 1.0, 0.0).astype(BF16)
    row0 = (_dot(hi, onehot) + _dot(mid, onehot) + _dot(lo, onehot)) * LOG2E
    r = lax.broadcasted_iota(jnp.int32, (Q_BLOCK, K_WINDOW), 0)
    c = lax.broadcasted_iota(jnp.int32, (Q_BLOCK, K_WINDOW), 1)
    band = ((r < CHUNK) & (c < ATT_PAST + CHUNK)) | ((r >= CHUNK) & (c >= CHUNK))
    for h in range(N_HEADS):
        x = pltpu.roll(jnp.broadcast_to(row0[h:h + 1], (Q_BLOCK, K_WINDOW)), 0, 1, stride=1, stride_axis=0)
        x = jnp.where(band, x, NEG_INF)
        bt_ref[h // 2, :, (h % 2) * Q_BLOCK:(h % 2 + 1) * Q_BLOCK] = x.T
        bs_ref[h] = x[:CHUNK]


def _attention_bias(rel_table):
    assert ATT_PAST >= REL_CLIP and rel_table.shape[2] == 2 * REL_CLIP + 1
    depth = rel_table.shape[0]
    table = jnp.pad(rel_table, ((0, 0), (0, 0), (0, TABLE_PAD - rel_table.shape[2])))
    bt, bs = pl.pallas_call(
        _bias_kernel,
        grid=(depth,),
        in_specs=[pl.BlockSpec((None, N_HEADS, TABLE_PAD), lambda l: (l, 0, 0))],
        out_specs=[pl.BlockSpec((None, N_PAIRS, K_WINDOW, 2 * Q_BLOCK), lambda l: (l, 0, 0, 0)),
                   pl.BlockSpec((None, N_HEADS, CHUNK, K_WINDOW), lambda l: (l, 0, 0, 0))],
        out_shape=[jax.ShapeDtypeStruct((depth, N_PAIRS, K_WINDOW, 2 * Q_BLOCK), F32),
                   jax.ShapeDtypeStruct((depth, N_HEADS, CHUNK, K_WINDOW), F32)],
        compiler_params=_params(1),
        name="rel_bias",
    )(table)
    return bt, bs.reshape(depth, N_PAIRS, 2 * CHUNK, K_WINDOW)


def _stack_heads(q_blk):
    lane = lax.broadcasted_iota(jnp.int32, q_blk.shape, 1)
    zero = jnp.zeros_like(q_blk)
    return jnp.concatenate([jnp.where(lane < HEAD_DIM, q_blk, zero), jnp.where(lane >= HEAD_DIM, q_blk, zero)], axis=0)


def _scores(q_blk, k_win):
    return lax.dot_general(_stack_heads(q_blk), k_win, (((1,), (1,)), ((), ())), preferred_element_type=F32)


def _softmax_numerator(s):
    return jnp.exp2(s - jnp.max(s, axis=-1, keepdims=True)).astype(BF16)


def _weighted_values(e, v_win):
    qb = e.shape[0] // 2
    o = _dot(e, jnp.concatenate([v_win, jnp.ones_like(v_win)], axis=1))
    o = o[:, :LANES] / o[:, LANES:]
    lane = lax.broadcasted_iota(jnp.int32, (qb, LANES), 1)
    return jnp.where(lane < HEAD_DIM, o[:qb], o[qb:])


def _merge_residual(x_ref, g_ref, c_ref, att, wao_ref, wout_ref):
    d = D_MODEL
    merged = g_ref[:, 0:d] * c_ref[...] + g_ref[:, d:2 * d] * _dot(att, _bf16(wao_ref))
    return x_ref[...] + _dot(merged.astype(BF16), _bf16(wout_ref))


def _attn_prompt_kernel(q_ref, kp_ref, kc_ref, vp_ref, vc_ref, bm_ref, x_ref, g_ref, c_ref, wao_ref, wout_ref, y_ref,
                        kcat_ref, vcat_ref, s_ref, e_ref, att_ref, *, tm):
    j = pl.program_id(1)
    n_qb = tm // Q_BLOCK
    n_steps = N_PAIRS * n_qb
    for hp in range(N_PAIRS):
        kcat_ref[hp, 0:tm] = kp_ref[hp]
        kcat_ref[hp, tm:2 * tm] = kc_ref[hp]
        vcat_ref[hp, :, 0:tm] = vp_ref[hp]
        vcat_ref[hp, :, tm:2 * tm] = vc_ref[hp]
    key_idx = lax.broadcasted_iota(jnp.int32, (K_WINDOW, 1), 0)
    ones = jnp.ones((2 * SUBLANES, K_WINDOW), BF16)

    def where(n):
        return n // n_qb, (n % n_qb) * Q_BLOCK

    def scores(n):
        hp, r0 = where(n)
        s_ref[n % PIPE] = lax.dot_general(kcat_ref[hp, r0:r0 + K_WINDOW, :], _stack_heads(q_ref[hp, r0:r0 + Q_BLOCK, :]),
                                          (((1,), (1,)), ((), ())), preferred_element_type=F32)

    def softmax(n):
        hp, r0 = where(n)
        key_mask = jnp.where((j == 0) & (key_idx + r0 < tm), NEG_INF, 0.0).astype(F32)
        s = s_ref[n % PIPE] + bm_ref[hp] + key_mask
        e_ref[n % PIPE] = jnp.exp2(s - jnp.max(s, axis=0, keepdims=True)).astype(BF16)

    def values(n):
        hp, r0 = where(n)
        vw = jnp.concatenate([vcat_ref[hp, :, r0:r0 + K_WINDOW], ones], axis=0)
        o = _dot(vw, e_ref[n % PIPE])
        o = o[:LANES] / o[LANES:LANES + 1]
        row = lax.broadcasted_iota(jnp.int32, (LANES, Q_BLOCK), 0)
        att_ref[hp, r0:r0 + Q_BLOCK, :] = jnp.where(row < HEAD_DIM, o[:, :Q_BLOCK], o[:, Q_BLOCK:]).T

    lag = PIPE // 2
    for n in range(-lag, n_steps + lag):
        for stage, k in ((scores, n + lag), (softmax, n), (values, n - lag)):
            if 0 <= k < n_steps:
                stage(k)
    att = jnp.concatenate([att_ref[hp].astype(BF16) for hp in range(N_PAIRS)], axis=1)
    y_ref[...] = _merge_residual(x_ref, g_ref, c_ref, att, wao_ref, wout_ref)


def _attn_prompt(q, k, v, bias, x, g, c, w_att_out, w_out, layer, n_seq, tm=ATT_PAST):
    _, t, _ = q.shape
    d = D_MODEL
    nt = t // (n_seq * tm)
    cur = lambda b, j: (0, b * nt + j, 0)
    prev = lambda b, j: (0, b * nt + jnp.maximum(j - 1, 0), 0)
    cur_t = lambda b, j: (0, 0, b * nt + j)
    prev_t = lambda b, j: (0, 0, b * nt + jnp.maximum(j - 1, 0))
    row = lambda b, j: (b * nt + j, 0)
    blk = (N_PAIRS, tm, LANES)
    blk_t = (N_PAIRS, LANES, tm)
    return pl.pallas_call(
        functools.partial(_attn_prompt_kernel, tm=tm),
        grid=(n_seq, nt),
        in_specs=[pl.BlockSpec(blk, cur), pl.BlockSpec(blk, prev), pl.BlockSpec(blk, cur),
                  pl.BlockSpec(blk_t, prev_t), pl.BlockSpec(blk_t, cur_t), _layer_spec(bias.shape[1:], layer),
                  pl.BlockSpec((tm, d), row), pl.BlockSpec((tm, 2 * d), row), pl.BlockSpec((tm, d), row),
                  _layer_spec((d, d), layer), _layer_spec((d, d), layer)],
        out_specs=pl.BlockSpec((tm, d), row),
        out_shape=jax.ShapeDtypeStruct((t, d), F32),
        scratch_shapes=[pltpu.VMEM((N_PAIRS, 2 * tm, LANES), BF16), pltpu.VMEM((N_PAIRS, LANES, 2 * tm), BF16),
                        pltpu.VMEM((PIPE, K_WINDOW, 2 * Q_BLOCK), F32), pltpu.VMEM((PIPE, K_WINDOW, 2 * Q_BLOCK), BF16),
                        pltpu.VMEM((N_PAIRS, tm, LANES), F32)],
        compiler_params=_params(2),
        name="attn_prompt",
    )(q, k, k, v, v, bias, x, g, c, w_att_out, w_out)


def _attn_sample_kernel(q_ref, kc_ref, kn_ref, knf_ref, vc_ref, vn_ref, vnf_ref, bm_ref, x_ref, g_ref, c_ref,
                        wao_ref, wout_ref, y_ref, ko_ref, vo_ref, kcat_ref, vcat_ref, *, seqs):
    l, t = ATT_PAST, CHUNK
    pad = jnp.zeros((K_WINDOW - l - t, LANES), BF16)
    att_rows = []
    for s in range(seqs):
        rs = slice(s * t, (s + 1) * t)
        for cache_ref, new_ref, new_f32_ref, out_ref, cat_ref in ((kc_ref, kn_ref, knf_ref, ko_ref, kcat_ref),
                                                                  (vc_ref, vn_ref, vnf_ref, vo_ref, vcat_ref)):
            for hp in range(N_PAIRS):
                cat_ref[hp, 0:l] = cache_ref[s, :, hp * LANES:(hp + 1) * LANES].astype(BF16)
                cat_ref[hp, l:l + t] = new_ref[hp, rs, :]
                cat_ref[hp, l + t:K_WINDOW] = pad
            out_ref[s, 0:l - t] = cache_ref[s, t:l]
            out_ref[s, l - t:l] = new_f32_ref[rs, :]
        att = []
        for hp in range(N_PAIRS):
            e = _softmax_numerator(_scores(q_ref[hp, rs, :], kcat_ref[hp]) + bm_ref[hp])
            att.append(_weighted_values(e, vcat_ref[hp]).astype(BF16))
        att_rows.append(jnp.concatenate(att, axis=1))
    y_ref[...] = _merge_residual(x_ref, g_ref, c_ref, jnp.concatenate(att_rows, axis=0), wao_ref, wout_ref)


def _attn_sample(q, cache_k, k_new, k_new_f32, cache_v, v_new, v_new_f32, bias, x, g, c, w_att_out, w_out, layer):
    _, t, _ = q.shape
    d = D_MODEL
    n_seq = t // CHUNK
    seqs = SAMPLE_SEQS
    assert n_seq % seqs == 0
    rows = seqs * CHUNK
    row = lambda b: (b, 0)
    pairs = pl.BlockSpec((N_PAIRS, rows, LANES), lambda b: (0, b, 0))
    cache = pl.BlockSpec((None, seqs, ATT_PAST, d), lambda b: (layer, b, 0, 0))
    new_cache = pl.BlockSpec((seqs, ATT_PAST, d), lambda b: (b, 0, 0))
    return pl.pallas_call(
        functools.partial(_attn_sample_kernel, seqs=seqs),
        grid=(n_seq // seqs,),
        in_specs=[pairs, cache, pairs, pl.BlockSpec((rows, d), row), cache, pairs, pl.BlockSpec((rows, d), row),
                  _layer_spec(bias.shape[1:], layer),
                  pl.BlockSpec((rows, d), row), pl.BlockSpec((rows, 2 * d), row), pl.BlockSpec((rows, d), row),
                  _layer_spec((d, d), layer), _layer_spec((d, d), layer)],
        out_specs=[pl.BlockSpec((rows, d), row), new_cache, new_cache],
        out_shape=[jax.ShapeDtypeStruct((t, d), F32), jax.ShapeDtypeStruct((n_seq, ATT_PAST, d), F32),
                   jax.ShapeDtypeStruct((n_seq, ATT_PAST, d), F32)],
        scratch_shapes=[pltpu.VMEM((N_PAIRS, K_WINDOW, LANES), BF16), pltpu.VMEM((N_PAIRS, K_WINDOW, LANES), BF16)],
        compiler_params=_params(1),
        name="attn_sample",
    )(q, cache_k, k_new, k_new_f32, cache_v, v_new, v_new_f32, bias, x, g, c, w_att_out, w_out)


def _ffn_kernel(x_ref, p_ref, fh_ref, nf_ref, wup_ref, fw_ref, fb_ref, wdn_ref,
                npl_ref, wpg_ref, wpp_ref, nfin_ref, y_ref, fst_ref, up_ref, *, tm, seg_rows, tiles_per_seq, final):
    i = pl.program_id(0)
    seqs = tm // seg_rows
    stride = FFN_HIST + seg_rows

    if seqs == 1:
        @pl.when(i % tiles_per_seq == 0)
        def _():
            up_ref[0:FFN_HIST] = fh_ref[0]

        @pl.when(i % tiles_per_seq != 0)
        def _():
            up_ref[0:FFN_HIST] = up_ref[tm:tm + FFN_HIST]
    else:
        for s in range(seqs):
            up_ref[s * stride:s * stride + FFN_HIST] = fh_ref[s]

    x = x_ref[...]
    h = _rms(x, nf_ref[...]).astype(BF16)
    up = _dot(h, wup_ref[:, 0:D_FF])
    for s in range(seqs):
        up_ref[s * stride + FFN_HIST:(s + 1) * stride] = up[s * seg_rows:(s + 1) * seg_rows]
    gate = _dot(h, wup_ref[:, D_FF:2 * D_FF])
    cvs = []
    for s in range(seqs):
        fst_ref[s] = up_ref[(s + 1) * stride - FFN_HIST:(s + 1) * stride]
        cv = fb_ref[...]
        for tap in range(FFN_W):
            first = s * stride + FFN_HIST - (FFN_W - 1) + tap
            cv = cv + fw_ref[tap:tap + 1] * up_ref[first:first + seg_rows]
        cvs.append(cv)
    cv = cvs[0] if seqs == 1 else jnp.concatenate(cvs, axis=0)
    act = cv * (lax.erf(cv / jnp.sqrt(F32(2.0))) + 1.0) / 2.0 * gate
    x = x + _dot(act.astype(BF16), _bf16(wdn_ref))

    h = _rms(x, npl_ref[...]).astype(BF16)
    x = x + jax.nn.sigmoid(_dot(h, _bf16(wpg_ref))) * _dot(p_ref[...].astype(BF16), _bf16(wpp_ref))
    y_ref[...] = _rms(x, nfin_ref[...]) if final else x


def _ffn(x, p, ffn_hist, hist_layer, norm_ffn, w_up, ffn_dw, ffn_b, w_down, norm_ple, w_pg, w_pp,
         norm_final, layer, seg, tm, final):
    t, d = x.shape
    n_tiles = t // tm
    seg_rows = min(seg, tm)
    seqs = tm // seg_rows
    tiles_per_seq = seg // seg_rows
    per_seq_hist = ffn_hist.shape[1] > 1
    row = lambda i: (i, 0)
    seq_blk = lambda i: (i // tiles_per_seq, 0, 0)
    kern = functools.partial(_ffn_kernel, tm=tm, seg_rows=seg_rows, tiles_per_seq=tiles_per_seq, final=final)
    return pl.pallas_call(
        kern,
        grid=(n_tiles,),
        in_specs=[pl.BlockSpec((tm, d), row), pl.BlockSpec((None, tm, PLE_DIM), lambda i: (layer, i, 0)),
                  pl.BlockSpec((None, seqs, FFN_HIST, D_FF),
                               lambda i: (hist_layer, (i // tiles_per_seq) if per_seq_hist else 0, 0, 0)),
                  _layer_spec((1, d), layer), _layer_spec((d, 2 * D_FF), layer),
                  _layer_spec((FFN_W, D_FF), layer), _layer_spec((1, D_FF), layer), _layer_spec((D_FF, d), layer),
                  _layer_spec((1, d), layer), _layer_spec((d, d), layer), _layer_spec((PLE_DIM, d), layer),
                  _const_spec((1, d))],
        out_specs=[pl.BlockSpec((tm, d), row), pl.BlockSpec((seqs, FFN_HIST, D_FF), seq_blk)],
        out_shape=[jax.ShapeDtypeStruct((t, d), F32),
                   jax.ShapeDtypeStruct((n_tiles * seqs // tiles_per_seq, FFN_HIST, D_FF), F32)],
        scratch_shapes=[pltpu.VMEM((seqs * (FFN_HIST + seg_rows), D_FF), F32)],
        compiler_params=_params(1),
        name="ffn",
    )(x, p, ffn_hist, norm_ffn, w_up, ffn_dw, ffn_b, w_down, norm_ple, w_pg, w_pp, norm_final)


def _pad_rows_front(h, rows):
    return jnp.pad(h, ((0, 0),) * (h.ndim - 2) + ((rows - h.shape[-2], 0), (0, 0)))


def kernel(x_prompt, x_sample, p_prompt, p_sample, cache_att_k, cache_att_v, state_conv, state_ffn_conv, norm_mix, w_in, conv_dw, conv_dw_b, conv_ln_g, conv_ln_b, w_conv_out, rel_table, w_att_out, b_gate, w_out, norm_ffn, w_ffn_up, ffn_dw, ffn_dw_b, w_ffn_down, norm_ple, w_ple_gate, w_ple_proj, norm_final):
    depth = w_in.shape[0]
    bp, sp, d = x_prompt.shape
    bs, ss, _ = x_sample.shape
    l_cache = cache_att_k.shape[2]
    xp = x_prompt.reshape(bp * sp, d)
    xs = x_sample.reshape(bs * ss, d)
    pp = p_prompt.reshape(depth, bp * sp, PLE_DIM)
    ps = p_sample.reshape(depth, bs * ss, PLE_DIM)
    ck = cache_att_k.reshape(depth, bs, l_cache, d)
    cv = cache_att_v.reshape(depth, bs, l_cache, d)
    rows = lambda v: v.reshape(v.shape[0], 1, v.shape[1])

    w_in_b, w_up_b = (w.astype(BF16) for w in (w_in, w_ffn_up))
    bias_p, bias_s = _attention_bias(rel_table)
    w_vt_b = jnp.swapaxes(w_in[:, :, 4 * d:5 * d], 1, 2).astype(BF16)
    conv_hist_s = _pad_rows_front(state_conv, CONV_HIST)
    ffn_hist_s = _pad_rows_front(state_ffn_conv, FFN_HIST)
    conv_hist_p = jnp.zeros((1, 1, CONV_HIST, d), F32)
    ffn_hist_p = jnp.zeros((1, 1, FFN_HIST, D_FF), F32)
    in_args = (rows(norm_mix), w_in_b, rows(b_gate))
    conv_args = (conv_dw, rows(conv_dw_b), rows(conv_ln_g), rows(conv_ln_b), w_conv_out)
    ffn_args = (rows(norm_ffn), w_up_b, ffn_dw, rows(ffn_dw_b), w_ffn_down, rows(norm_ple), w_ple_gate, w_ple_proj,
                norm_final.reshape(1, d))
    keep = min(ATT_PAST, sp)

    outs = [[] for _ in range(8)]
    for i in range(depth):
        final = i == depth - 1
        u, q, k, v, g, kst, vst = _in_proj(xp, *in_args, layer=i, tiles_per_seq=sp // ROW_TILE, w_vt=w_vt_b)
        c = _conv_module(u, conv_hist_p, 0, *conv_args, layer=i, n_seq=bp, tm=ROW_TILE)
        xp = _attn_prompt(q, k, v, bias_p, xp, g, c, w_att_out, w_out, layer=i, n_seq=bp)
        xp, fst = _ffn(xp, pp, ffn_hist_p, 0, *ffn_args, layer=i, seg=sp, tm=ROW_TILE, final=final)
        outs[0].append(kst[:, ROW_TILE - keep:])
        outs[1].append(vst[:, ROW_TILE - keep:])
        outs[2].append(u.reshape(bp, sp, d)[:, sp - (CONV_W - 1):])
        outs[3].append(fst[:, FFN_HIST - (FFN_W - 1):])

        u, q, k, v, g, kst, vst = _in_proj(xs, *in_args, layer=i, tiles_per_seq=1)
        c = _conv_module(u, conv_hist_s, i, *conv_args, layer=i, n_seq=bs, tm=ss)
        xs, k_cache, v_cache = _attn_sample(q, ck, k, kst[0], cv, v, vst[0], bias_s, xs, g, c, w_att_out, w_out, layer=i)
        xs, fst = _ffn(xs, ps, ffn_hist_s, i, *ffn_args, layer=i, seg=ss, tm=bs * ss, final=final)
        outs[4].append(k_cache)
        outs[5].append(v_cache)
        outs[6].append(jnp.concatenate([state_conv[i], u.reshape(bs, ss, d)], axis=1)[:, ss:])
        outs[7].append(fst[:, FFN_HIST - (FFN_W - 1):])

    heads = lambda o: jnp.stack(o).reshape(depth, -1, o[0].shape[1], N_HEADS, HEAD_DIM)
    return (xp.reshape(bp, sp, d), xs.reshape(bs, ss, d), heads(outs[0]), heads(outs[1]), jnp.stack(outs[2]),
            jnp.stack(outs[3]), heads(outs[4]), heads(outs[5]), jnp.stack(outs[6]), jnp.stack(outs[7]))
```

```python
import functools
import math

import jax
import jax.numpy as jnp
from jax import lax
from jax.experimental import pallas as pl
from jax.experimental.pallas import tpu as pltpu

D_MODEL = 1024
N_HEADS = 16
HEAD_DIM = 64
CHUNK = 64
N_PAST_CHUNKS = 8
ATT_PAST = N_PAST_CHUNKS * CHUNK
REL_CLIP = 128
CONV_W = 31
D_FF = 2816
FFN_W = 3
PLE_DIM = 256
EPS = 1e-6
NEG_INF = -1e30

LANES = 128
SUBLANES = 8
N_PAIRS = D_MODEL // LANES
Q_BLOCK = 2 * CHUNK
K_WINDOW = ATT_PAST + Q_BLOCK
ROW_TILE = ATT_PAST
SAMPLE_SEQS = 2
PIPE = 4
CONV_HIST = 32
FFN_HIST = 8
TABLE_PAD = 384
VMEM_LIMIT = 60 * 1024 * 1024
LOG2E = math.log2(math.e)
Q_SCALE = HEAD_DIM ** -0.5 * LOG2E

BF16 = jnp.bfloat16
F32 = jnp.float32


def _const_spec(shape):
    zeros = (0,) * len(shape)
    return pl.BlockSpec(shape, lambda *_: zeros, pipeline_mode=pl.Buffered(1))


def _layer_spec(shape, layer):
    zeros = (0,) * len(shape)
    return pl.BlockSpec((None, *shape), lambda *_: (layer, *zeros), pipeline_mode=pl.Buffered(1))


def _params(n_axes):
    return pltpu.CompilerParams(dimension_semantics=("arbitrary",) * n_axes, vmem_limit_bytes=VMEM_LIMIT)


def _rms(x, w):
    return x * lax.rsqrt(jnp.mean(x * x, axis=-1, keepdims=True) + EPS) * w


def _dot(a, b):
    return jnp.dot(a, b, preferred_element_type=F32)


def _bf16(w_ref):
    return w_ref[...].astype(BF16)


def _in_proj_kernel(x_ref, nw_ref, w_ref, bg_ref, *rest, tiles_per_seq, v_transposed):
    if v_transposed:
        wvt_ref, u_ref, q_ref, k_ref, v_ref, g_ref, kst_ref, vst_ref = rest
    else:
        u_ref, q_ref, k_ref, v_ref, g_ref, kst_ref, vst_ref = rest
    d = D_MODEL
    h = _rms(x_ref[...], nw_ref[...]).astype(BF16)
    glu_a = _dot(h, w_ref[:, 0:d])
    glu_b = _dot(h, w_ref[:, d:2 * d])
    u_ref[...] = glu_a * jax.nn.sigmoid(glu_b)

    def put_pairs(dst_ref, val):
        for hp in range(N_PAIRS):
            dst_ref[hp] = val[:, hp * LANES:(hp + 1) * LANES]

    put_pairs(q_ref, (_dot(h, w_ref[:, 2 * d:3 * d]) * Q_SCALE).astype(BF16))
    k = _dot(h, w_ref[:, 3 * d:4 * d])
    put_pairs(k_ref, k.astype(BF16))
    kst_ref[0] = k
    if v_transposed:
        vt = lax.dot_general(wvt_ref[...], h, (((1,), (1,)), ((), ())), preferred_element_type=F32).astype(BF16)
        for hp in range(N_PAIRS):
            v_ref[hp] = vt[hp * LANES:(hp + 1) * LANES, :]
    else:
        v = _dot(h, w_ref[:, 4 * d:5 * d])
        put_pairs(v_ref, v.astype(BF16))
        vst_ref[0] = v
    g_ref[...] = jax.nn.sigmoid(_dot(h, w_ref[:, 5 * d:7 * d]) + bg_ref[...])
    if v_transposed:
        @pl.when(pl.program_id(0) % tiles_per_seq == tiles_per_seq - 1)
        def _():
            vst_ref[0] = _dot(h, w_ref[:, 4 * d:5 * d])


def _in_proj(x, norm_w, w_in, b_gate, layer, tiles_per_seq, tm=ROW_TILE, w_vt=None):
    t, d = x.shape
    n_tiles = t // tm
    n_seq = n_tiles // tiles_per_seq
    row = lambda i: (i, 0)
    st = lambda i: (i // tiles_per_seq, 0, 0)
    pairs = pl.BlockSpec((N_PAIRS, tm, LANES), lambda i: (0, i, 0))
    pairs_shape = jax.ShapeDtypeStruct((N_PAIRS, t, LANES), BF16)
    vt = w_vt is not None
    v_spec = pl.BlockSpec((N_PAIRS, LANES, tm), lambda i: (0, 0, i)) if vt else pairs
    v_shape = jax.ShapeDtypeStruct((N_PAIRS, LANES, t), BF16) if vt else pairs_shape
    return pl.pallas_call(
        functools.partial(_in_proj_kernel, tiles_per_seq=tiles_per_seq, v_transposed=vt),
        grid=(n_tiles,),
        in_specs=[pl.BlockSpec((tm, d), row), _layer_spec((1, d), layer), _layer_spec((d, 7 * d), layer),
                  _layer_spec((1, 2 * d), layer)] + ([_layer_spec((d, d), layer)] if vt else []),
        out_specs=[pl.BlockSpec((tm, d), row), pairs, pairs, v_spec, pl.BlockSpec((tm, 2 * d), row),
                   pl.BlockSpec((1, tm, d), st), pl.BlockSpec((1, tm, d), st)],
        out_shape=[jax.ShapeDtypeStruct((t, d), F32), pairs_shape, pairs_shape, v_shape,
                   jax.ShapeDtypeStruct((t, 2 * d), F32),
                   jax.ShapeDtypeStruct((n_seq, tm, d), F32), jax.ShapeDtypeStruct((n_seq, tm, d), F32)],
        compiler_params=_params(1),
        name="in_proj",
    )(x, norm_w, w_in, b_gate, *([w_vt] if vt else []))


def _conv_kernel(u_ref, hist_ref, w_ref, b_ref, lg_ref, lb_ref, wo_ref, c_ref, xe_ref, xs_ref, y_ref, *, tm, rows):
    j = pl.program_id(1)

    @pl.when(j == 0)
    def _():
        for cb in range(D_MODEL // LANES):
            xe_ref[cb, 0:CONV_HIST] = hist_ref[:, cb * LANES:(cb + 1) * LANES]

    @pl.when(j > 0)
    def _():
        for cb in range(D_MODEL // LANES):
            xe_ref[cb, 0:CONV_HIST] = xe_ref[cb, tm:tm + CONV_HIST]

    n_shift = CONV_HIST + tm - SUBLANES
    first = CONV_HIST - (CONV_W - 1)
    for cb in range(D_MODEL // LANES):
        cs = slice(cb * LANES, (cb + 1) * LANES)
        xe_ref[cb, CONV_HIST:CONV_HIST + tm] = u_ref[:, cs]
        for r in range(1, SUBLANES):
            xs_ref[cb, r - 1] = xe_ref[cb, r:r + n_shift]

        for r0 in range(0, tm, rows):
            acc = jnp.broadcast_to(b_ref[:, cs], (rows, LANES))
            for tap in range(CONV_W):
                shift = (first + tap) % SUBLANES
                base = first + tap - shift
                if shift == 0:
                    win = xe_ref[cb, r0 + base:r0 + base + rows, :]
                else:
                    win = xs_ref[cb, shift - 1, r0 + base:r0 + base + rows, :]
                acc = acc + w_ref[tap:tap + 1, cs] * win
            y_ref[r0:r0 + rows, cs] = acc

    y = y_ref[...]
    mu = jnp.mean(y, axis=-1, keepdims=True)
    yc = y - mu
    var = jnp.mean(yc * yc, axis=-1, keepdims=True)
    z = yc * lax.rsqrt(var + EPS) * lg_ref[...] + lb_ref[...]
    c_ref[...] = _dot(jax.nn.silu(z).astype(BF16), _bf16(wo_ref))


def _conv_module(u, hist, hist_layer, conv_dw, conv_b, ln_g, ln_b, w_conv_out, layer, n_seq, tm, rows=64):
    t, d = u.shape
    nt = t // (n_seq * tm)
    per_seq_hist = hist.shape[1] > 1
    kern = functools.partial(_conv_kernel, tm=tm, rows=rows)
    row = lambda b, j: (b * nt + j, 0)
    return pl.pallas_call(
        kern,
        grid=(n_seq, nt),
        in_specs=[pl.BlockSpec((tm, d), row),
                  pl.BlockSpec((None, None, CONV_HIST, d), lambda b, j: (hist_layer, b if per_seq_hist else 0, 0, 0)),
                  _layer_spec((CONV_W, d), layer), _layer_spec((1, d), layer), _layer_spec((1, d), layer),
                  _layer_spec((1, d), layer), _layer_spec((d, d), layer)],
        out_specs=pl.BlockSpec((tm, d), row),
        out_shape=jax.ShapeDtypeStruct((t, d), F32),
        scratch_shapes=[pltpu.VMEM((d // LANES, CONV_HIST + tm, LANES), F32),
                        pltpu.VMEM((d // LANES, SUBLANES - 1, CONV_HIST + tm - SUBLANES, LANES), F32),
                        pltpu.VMEM((tm, d), F32)],
        compiler_params=_params(2),
        name="conv_module",
    )(u, hist, conv_dw, conv_b, ln_g, ln_b, w_conv_out)


def _bias_kernel(t_ref, bt_ref, bs_ref):
    t = t_ref[...]
    hi = t.astype(BF16)
    rest = t - hi.astype(F32)
    mid = rest.astype(BF16)
    lo = (rest - mid.astype(F32)).astype(BF16)
    idx = lax.broadcasted_iota(jnp.int32, (TABLE_PAD, K_WINDOW), 0)
    col = lax.broadcasted_iota(jnp.int32, (TABLE_PAD, K_WINDOW), 1)
    sel = jnp.where(col < ATT_PAST + CHUNK, jnp.clip(ATT_PAST - col, -REL_CLIP, REL_CLIP) + REL_CLIP, 2 * REL_CLIP)
    onehot = jnp.where(idx == sel, 1.0, 0.0).astype(BF16)
    row0 = (_dot(hi, onehot) + _dot(mid, onehot) + _dot(lo, onehot)) * LOG2E
    r = lax.broadcasted_iota(jnp.int32, (Q_BLOCK, K_WINDOW), 0)
    c = lax.broadcasted_iota(jnp.int32, (Q_BLOCK, K_WINDOW), 1)
    band = ((r < CHUNK) & (c < ATT_PAST + CHUNK)) | ((r >= CHUNK) & (c >= CHUNK))
    for h in range(N_HEADS):
        x = pltpu.roll(jnp.broadcast_to(row0[h:h + 1], (Q_BLOCK, K_WINDOW)), 0, 1, stride=1, stride_axis=0)
        x = jnp.where(band, x, NEG_INF)
        bt_ref[h // 2, :, (h % 2) * Q_BLOCK:(h % 2 + 1) * Q_BLOCK] = x.T
        bs_ref[h] = x[:CHUNK]


def _attention_bias(rel_table):
    assert ATT_PAST >= REL_CLIP and rel_table.shape[2] == 2 * REL_CLIP + 1
    depth = rel_table.shape[0]
    table = jnp.pad(rel_table, ((0, 0), (0, 0), (0, TABLE_PAD - rel_table.shape[2])))
    bt, bs = pl.pallas_call(
        _bias_kernel,
        grid=(depth,),
        in_specs=[pl.BlockSpec((None, N_HEADS, TABLE_PAD), lambda l: (l, 0, 0))],
        out_specs=[pl.BlockSpec((None, N_PAIRS, K_WINDOW, 2 * Q_BLOCK), lambda l: (l, 0, 0, 0)),
                   pl.BlockSpec((None, N_HEADS, CHUNK, K_WINDOW), lambda l: (l, 0, 0, 0))],
        out_shape=[jax.ShapeDtypeStruct((depth, N_PAIRS, K_WINDOW, 2 * Q_BLOCK), F32),
                   jax.ShapeDtypeStruct((depth, N_HEADS, CHUNK, K_WINDOW), F32)],
        compiler_params=_params(1),
        name="rel_bias",
    )(table)
    return bt, bs.reshape(depth, N_PAIRS, 2 * CHUNK, K_WINDOW)


def _stack_heads(q_blk):
    lane = lax.broadcasted_iota(jnp.int32, q_blk.shape, 1)
    zero = jnp.zeros_like(q_blk)
    return jnp.concatenate([jnp.where(lane < HEAD_DIM, q_blk, zero), jnp.where(lane >= HEAD_DIM, q_blk, zero)], axis=0)


def _scores(q_blk, k_win):
    return lax.dot_general(_stack_heads(q_blk), k_win, (((1,), (1,)), ((), ())), preferred_element_type=F32)


def _softmax_numerator(s):
    return jnp.exp2(s - jnp.max(s, axis=-1, keepdims=True)).astype(BF16)


def _weighted_values(e, v_win):
    qb = e.shape[0] // 2
    o = _dot(e, jnp.concatenate([v_win, jnp.ones_like(v_win)], axis=1))
    o = o[:, :LANES] / o[:, LANES:]
    lane = lax.broadcasted_iota(jnp.int32, (qb, LANES), 1)
    return jnp.where(lane < HEAD_DIM, o[:qb], o[qb:])


def _merge_residual(x_ref, g_ref, c_ref, att, wao_ref, wout_ref):
    d = D_MODEL
    merged = g_ref[:, 0:d] * c_ref[...] + g_ref[:, d:2 * d] * _dot(att, _bf16(wao_ref))
    return x_ref[...] + _dot(merged.astype(BF16), _bf16(wout_ref))


def _attn_prompt_kernel(q_ref, kp_ref, kc_ref, vp_ref, vc_ref, bm_ref, x_ref, g_ref, c_ref, wao_ref, wout_ref, y_ref,
                        kcat_ref, vcat_ref, s_ref, e_ref, att_ref, *, tm):
    j = pl.program_id(1)
    n_qb = tm // Q_BLOCK
    n_steps = N_PAIRS * n_qb
    for hp in range(N_PAIRS):
        kcat_ref[hp, 0:tm] = kp_ref[hp]
        kcat_ref[hp, tm:2 * tm] = kc_ref[hp]
        vcat_ref[hp, :, 0:tm] = vp_ref[hp]
        vcat_ref[hp, :, tm:2 * tm] = vc_ref[hp]
    key_idx = lax.broadcasted_iota(jnp.int32, (K_WINDOW, 1), 0)
    ones = jnp.ones((2 * SUBLANES, K_WINDOW), BF16)

    def where(n):
        return n // n_qb, (n % n_qb) * Q_BLOCK

    def scores(n):
        hp, r0 = where(n)
        s_ref[n % PIPE] = lax.dot_general(kcat_ref[hp, r0:r0 + K_WINDOW, :], _stack_heads(q_ref[hp, r0:r0 + Q_BLOCK, :]),
                                          (((1,), (1,)), ((), ())), preferred_element_type=F32)

    def softmax(n):
        hp, r0 = where(n)
        key_mask = jnp.where((j == 0) & (key_idx + r0 < tm), NEG_INF, 0.0).astype(F32)
        for half in range(2):
            cols = slice(half * Q_BLOCK, (half + 1) * Q_BLOCK)
            s = s_ref[n % PIPE, :, cols] + bm_ref[hp, :, cols] + key_mask
            e_ref[n % PIPE, :, cols] = jnp.exp2(s - jnp.max(s, axis=0, keepdims=True)).astype(BF16)

    def values(n):
        hp, r0 = where(n)
        vw = jnp.concatenate([vcat_ref[hp, :, r0:r0 + K_WINDOW], ones], axis=0)
        o = _dot(vw, e_ref[n % PIPE])
        o = o[:LANES] / o[LANES:LANES + 1]
        row = lax.broadcasted_iota(jnp.int32, (LANES, Q_BLOCK), 0)
        att_ref[hp, r0:r0 + Q_BLOCK, :] = jnp.where(row < HEAD_DIM, o[:, :Q_BLOCK], o[:, Q_BLOCK:]).T

    lag = PIPE // 2
    for n in range(-lag, n_steps + lag):
        for stage, k in ((scores, n + lag), (softmax, n), (values, n - lag)):
            if 0 <= k < n_steps:
                stage(k)
    att = jnp.concatenate([att_ref[hp].astype(BF16) for hp in range(N_PAIRS)], axis=1)
    y_ref[...] = _merge_residual(x_ref, g_ref, c_ref, att, wao_ref, wout_ref)


def _attn_prompt(q, k, v, bias, x, g, c, w_att_out, w_out, layer, n_seq, tm=ATT_PAST):
    _, t, _ = q.shape
    d = D_MODEL
    nt = t // (n_seq * tm)
    cur = lambda b, j: (0, b * nt + j, 0)
    prev = lambda b, j: (0, b * nt + jnp.maximum(j - 1, 0), 0)
    cur_t = lambda b, j: (0, 0, b * nt + j)
    prev_t = lambda b, j: (0, 0, b * nt + jnp.maximum(j - 1, 0))
    row = lambda b, j: (b * nt + j, 0)
    blk = (N_PAIRS, tm, LANES)
    blk_t = (N_PAIRS, LANES, tm)
    return pl.pallas_call(
        functools.partial(_attn_prompt_kernel, tm=tm),
        grid=(n_seq, nt),
        in_specs=[pl.BlockSpec(blk, cur), pl.BlockSpec(blk, prev), pl.BlockSpec(blk, cur),
                  pl.BlockSpec(blk_t, prev_t), pl.BlockSpec(blk_t, cur_t), _layer_spec(bias.shape[1:], layer),
                  pl.BlockSpec((tm, d), row), pl.BlockSpec((tm, 2 * d), row), pl.BlockSpec((tm, d), row),
                  _layer_spec((d, d), layer), _layer_spec((d, d), layer)],
        out_specs=pl.BlockSpec((tm, d), row),
        out_shape=jax.ShapeDtypeStruct((t, d), F32),
        scratch_shapes=[pltpu.VMEM((N_PAIRS, 2 * tm, LANES), BF16), pltpu.VMEM((N_PAIRS, LANES, 2 * tm), BF16),
                        pltpu.VMEM((PIPE, K_WINDOW, 2 * Q_BLOCK), F32), pltpu.VMEM((PIPE, K_WINDOW, 2 * Q_BLOCK), BF16),
                        pltpu.VMEM((N_PAIRS, tm, LANES), F32)],
        compiler_params=_params(2),
        name="attn_prompt",
    )(q, k, k, v, v, bias, x, g, c, w_att_out, w_out)


def _attn_sample_kernel(q_ref, kc_ref, kn_ref, knf_ref, vc_ref, vn_ref, vnf_ref, bm_ref, x_ref, g_ref, c_ref,
                        wao_ref, wout_ref, y_ref, ko_ref, vo_ref, kcat_ref, vcat_ref, *, seqs):
    l, t = ATT_PAST, CHUNK
    pad = jnp.zeros((K_WINDOW - l - t, LANES), BF16)
    att_rows = []
    for s in range(seqs):
        rs = slice(s * t, (s + 1) * t)
        for cache_ref, new_ref, new_f32_ref, out_ref, cat_ref in ((kc_ref, kn_ref, knf_ref, ko_ref, kcat_ref),
                                                                  (vc_ref, vn_ref, vnf_ref, vo_ref, vcat_ref)):
            for hp in range(N_PAIRS):
                cat_ref[hp, 0:l] = cache_ref[s, :, hp * LANES:(hp + 1) * LANES].astype(BF16)
                cat_ref[hp, l:l + t] = new_ref[hp, rs, :]
                cat_ref[hp, l + t:K_WINDOW] = pad
            out_ref[s, 0:l - t] = cache_ref[s, t:l]
            out_ref[s, l - t:l] = new_f32_ref[rs, :]
        att = []
        for hp in range(N_PAIRS):
            e = _softmax_numerator(_scores(q_ref[hp, rs, :], kcat_ref[hp]) + bm_ref[hp])
            att.append(_weighted_values(e, vcat_ref[hp]).astype(BF16))
        att_rows.append(jnp.concatenate(att, axis=1))
    y_ref[...] = _merge_residual(x_ref, g_ref, c_ref, jnp.concatenate(att_rows, axis=0), wao_ref, wout_ref)


def _attn_sample(q, cache_k, k_new, k_new_f32, cache_v, v_new, v_new_f32, bias, x, g, c, w_att_out, w_out, layer):
    _, t, _ = q.shape
    d = D_MODEL
    n_seq = t // CHUNK
    seqs = SAMPLE_SEQS
    assert n_seq % seqs == 0
    rows = seqs * CHUNK
    row = lambda b: (b, 0)
    pairs = pl.BlockSpec((N_PAIRS, rows, LANES), lambda b: (0, b, 0))
    cache = pl.BlockSpec((None, seqs, ATT_PAST, d), lambda b: (layer, b, 0, 0))
    new_cache = pl.BlockSpec((seqs, ATT_PAST, d), lambda b: (b, 0, 0))
    return pl.pallas_call(
        functools.partial(_attn_sample_kernel, seqs=seqs),
        grid=(n_seq // seqs,),
        in_specs=[pairs, cache, pairs, pl.BlockSpec((rows, d), row), cache, pairs, pl.BlockSpec((rows, d), row),
                  _layer_spec(bias.shape[1:], layer),
                  pl.BlockSpec((rows, d), row), pl.BlockSpec((rows, 2 * d), row), pl.BlockSpec((rows, d), row),
                  _layer_spec((d, d), layer), _layer_spec((d, d), layer)],
        out_specs=[pl.BlockSpec((rows, d), row), new_cache, new_cache],
        out_shape=[jax.ShapeDtypeStruct((t, d), F32), jax.ShapeDtypeStruct((n_seq, ATT_PAST, d), F32),
                   jax.ShapeDtypeStruct((n_seq, ATT_PAST, d), F32)],
        scratch_shapes=[pltpu.VMEM((N_PAIRS, K_WINDOW, LANES), BF16), pltpu.VMEM((N_PAIRS, K_WINDOW, LANES), BF16)],
        compiler_params=_params(1),
        name="attn_sample",
    )(q, cache_k, k_new, k_new_f32, cache_v, v_new, v_new_f32, bias, x, g, c, w_att_out, w_out)


def _ffn_kernel(x_ref, p_ref, fh_ref, nf_ref, wup_ref, fw_ref, fb_ref, wdn_ref,
                npl_ref, wpg_ref, wpp_ref, nfin_ref, y_ref, fst_ref, up_ref, *, tm, seg_rows, tiles_per_seq, final):
    i = pl.program_id(0)
    seqs = tm // seg_rows
    stride = FFN_HIST + seg_rows

    if seqs == 1:
        @pl.when(i % tiles_per_seq == 0)
        def _():
            up_ref[0:FFN_HIST] = fh_ref[0]

        @pl.when(i % tiles_per_seq != 0)
        def _():
            up_ref[0:FFN_HIST] = up_ref[tm:tm + FFN_HIST]
    else:
        for s in range(seqs):
            up_ref[s * stride:s * stride + FFN_HIST] = fh_ref[s]

    x = x_ref[...]
    h = _rms(x, nf_ref[...]).astype(BF16)
    up = _dot(h, wup_ref[:, 0:D_FF])
    for s in range(seqs):
        up_ref[s * stride + FFN_HIST:(s + 1) * stride] = up[s * seg_rows:(s + 1) * seg_rows]
    gate = _dot(h, wup_ref[:, D_FF:2 * D_FF])
    cvs = []
    for s in range(seqs):
        fst_ref[s] = up_ref[(s + 1) * stride - FFN_HIST:(s + 1) * stride]
        cv = fb_ref[...]
        for tap in range(FFN_W):
            first = s * stride + FFN_HIST - (FFN_W - 1) + tap
            cv = cv + fw_ref[tap:tap + 1] * up_ref[first:first + seg_rows]
        cvs.append(cv)
    cv = cvs[0] if seqs == 1 else jnp.concatenate(cvs, axis=0)
    act = cv * (lax.erf(cv / jnp.sqrt(F32(2.0))) + 1.0) / 2.0 * gate
    x = x + _dot(act.astype(BF16), _bf16(wdn_ref))

    h = _rms(x, npl_ref[...]).astype(BF16)
    x = x + jax.nn.sigmoid(_dot(h, _bf16(wpg_ref))) * _dot(p_ref[...].astype(BF16), _bf16(wpp_ref))
    y_ref[...] = _rms(x, nfin_ref[...]) if final else x


def _ffn(x, p, ffn_hist, hist_layer, norm_ffn, w_up, ffn_dw, ffn_b, w_down, norm_ple, w_pg, w_pp,
         norm_final, layer, seg, tm, final):
    t, d = x.shape
    n_tiles = t // tm
    seg_rows = min(seg, tm)
    seqs = tm // seg_rows
    tiles_per_seq = seg // seg_rows
    per_seq_hist = ffn_hist.shape[1] > 1
    row = lambda i: (i, 0)
    seq_blk = lambda i: (i // tiles_per_seq, 0, 0)
    kern = functools.partial(_ffn_kernel, tm=tm, seg_rows=seg_rows, tiles_per_seq=tiles_per_seq, final=final)
    return pl.pallas_call(
        kern,
        grid=(n_tiles,),
        in_specs=[pl.BlockSpec((tm, d), row), pl.BlockSpec((None, tm, PLE_DIM), lambda i: (layer, i, 0)),
                  pl.BlockSpec((None, seqs, FFN_HIST, D_FF),
                               lambda i: (hist_layer, (i // tiles_per_seq) if per_seq_hist else 0, 0, 0)),
                  _layer_spec((1, d), layer), _layer_spec((d, 2 * D_FF), layer),
                  _layer_spec((FFN_W, D_FF), layer), _layer_spec((1, D_FF), layer), _layer_spec((D_FF, d), layer),
                  _layer_spec((1, d), layer), _layer_spec((d, d), layer), _layer_spec((PLE_DIM, d), layer),
                  _const_spec((1, d))],
        out_specs=[pl.BlockSpec((tm, d), row), pl.BlockSpec((seqs, FFN_HIST, D_FF), seq_blk)],
        out_shape=[jax.ShapeDtypeStruct((t, d), F32),
                   jax.ShapeDtypeStruct((n_tiles * seqs // tiles_per_seq, FFN_HIST, D_FF), F32)],
        scratch_shapes=[pltpu.VMEM((seqs * (FFN_HIST + seg_rows), D_FF), F32)],
        compiler_params=_params(1),
        name="ffn",
    )(x, p, ffn_hist, norm_ffn, w_up, ffn_dw, ffn_b, w_down, norm_ple, w_pg, w_pp, norm_final)


def _pad_rows_front(h, rows):
    return jnp.pad(h, ((0, 0),) * (h.ndim - 2) + ((rows - h.shape[-2], 0), (0, 0)))


def kernel(x_prompt, x_sample, p_prompt, p_sample, cache_att_k, cache_att_v, state_conv, state_ffn_conv, norm_mix, w_in, conv_dw, conv_dw_b, conv_ln_g, conv_ln_b, w_conv_out, rel_table, w_att_out, b_gate, w_out, norm_ffn, w_ffn_up, ffn_dw, ffn_dw_b, w_ffn_down, norm_ple, w_ple_gate, w_ple_proj, norm_final):
    depth = w_in.shape[0]
    bp, sp, d = x_prompt.shape
    bs, ss, _ = x_sample.shape
    l_cache = cache_att_k.shape[2]
    xp = x_prompt.reshape(bp * sp, d)
    xs = x_sample.reshape(bs * ss, d)
    pp = p_prompt.reshape(depth, bp * sp, PLE_DIM)
    ps = p_sample.reshape(depth, bs * ss, PLE_DIM)
    ck = cache_att_k.reshape(depth, bs, l_cache, d)
    cv = cache_att_v.reshape(depth, bs, l_cache, d)
    rows = lambda v: v.reshape(v.shape[0], 1, v.shape[1])

    w_in_b, w_up_b = (w.astype(BF16) for w in (w_in, w_ffn_up))
    bias_p, bias_s = _attention_bias(rel_table)
    w_vt_b = jnp.swapaxes(w_in[:, :, 4 * d:5 * d], 1, 2).astype(BF16)
    conv_hist_s = _pad_rows_front(state_conv, CONV_HIST)
    ffn_hist_s = _pad_rows_front(state_ffn_conv, FFN_HIST)
    conv_hist_p = jnp.zeros((1, 1, CONV_HIST, d), F32)
    ffn_hist_p = jnp.zeros((1, 1, FFN_HIST, D_FF), F32)
    in_args = (rows(norm_mix), w_in_b, rows(b_gate))
    conv_args = (conv_dw, rows(conv_dw_b), rows(conv_ln_g), rows(conv_ln_b), w_conv_out)
    ffn_args = (rows(norm_ffn), w_up_b, ffn_dw, rows(ffn_dw_b), w_ffn_down, rows(norm_ple), w_ple_gate, w_ple_proj,
                norm_final.reshape(1, d))
    keep = min(ATT_PAST, sp)

    outs = [[] for _ in range(8)]
    for i in range(depth):
        final = i == depth - 1
        u, q, k, v, g, kst, vst = _in_proj(xp, *in_args, layer=i, tiles_per_seq=sp // ROW_TILE, w_vt=w_vt_b)
        c = _conv_module(u, conv_hist_p, 0, *conv_args, layer=i, n_seq=bp, tm=ROW_TILE)
        xp = _attn_prompt(q, k, v, bias_p, xp, g, c, w_att_out, w_out, layer=i, n_seq=bp)
        xp, fst = _ffn(xp, pp, ffn_hist_p, 0, *ffn_args, layer=i, seg=sp, tm=ROW_TILE, final=final)
        outs[0].append(kst[:, ROW_TILE - keep:])
        outs[1].append(vst[:, ROW_TILE - keep:])
        outs[2].append(u.reshape(bp, sp, d)[:, sp - (CONV_W - 1):])
        outs[3].append(fst[:, FFN_HIST - (FFN_W - 1):])

        u, q, k, v, g, kst, vst = _in_proj(xs, *in_args, layer=i, tiles_per_seq=1)
        c = _conv_module(u, conv_hist_s, i, *conv_args, layer=i, n_seq=bs, tm=ss)
        xs, k_cache, v_cache = _attn_sample(q, ck, k, kst[0], cv, v, vst[0], bias_s, xs, g, c, w_att_out, w_out, layer=i)
        xs, fst = _ffn(xs, ps, ffn_hist_s, i, *ffn_args, layer=i, seg=ss, tm=bs * ss, final=final)
        outs[4].append(k_cache)
        outs[5].append(v_cache)
        outs[6].append(jnp.concatenate([state_conv[i], u.reshape(bs, ss, d)], axis=1)[:, ss:])
        outs[7].append(fst[:, FFN_HIST - (FFN_W - 1):])

    heads = lambda o: jnp.stack(o).reshape(depth, -1, o[0].shape[1], N_HEADS, HEAD_DIM)
    return (xp.reshape(bp, sp, d), xs.reshape(bs, ss, d), heads(outs[0]), heads(outs[1]), jnp.stack(outs[2]),
            jnp.stack(outs[3]), heads(outs[4]), heads(outs[5]), jnp.stack(outs[6]), jnp.stack(outs[7]))
```

```python
import functools
import math

import jax
import jax.numpy as jnp
from jax import lax
from jax.experimental import pallas as pl
from jax.experimental.pallas import tpu as pltpu

D_MODEL = 1024
N_HEADS = 16
HEAD_DIM = 64
CHUNK = 64
N_PAST_CHUNKS = 8
ATT_PAST = N_PAST_CHUNKS * CHUNK
REL_CLIP = 128
CONV_W = 31
D_FF = 2816
FFN_W = 3
PLE_DIM = 256
EPS = 1e-6
NEG_INF = -1e30

LANES = 128
SUBLANES = 8
N_PAIRS = D_MODEL // LANES
Q_BLOCK = 2 * CHUNK
K_WINDOW = ATT_PAST + Q_BLOCK
ROW_TILE = ATT_PAST
SAMPLE_SEQS = 2
PIPE = 4
CONV_HIST = 32
FFN_HIST = 8
TABLE_PAD = 384
VMEM_LIMIT = 60 * 1024 * 1024
LOG2E = math.log2(math.e)
Q_SCALE = HEAD_DIM ** -0.5 * LOG2E

BF16 = jnp.bfloat16
F32 = jnp.float32


def _const_spec(shape):
    zeros = (0,) * len(shape)
    return pl.BlockSpec(shape, lambda *_: zeros, pipeline_mode=pl.Buffered(1))


def _layer_spec(shape, layer):
    zeros = (0,) * len(shape)
    return pl.BlockSpec((None, *shape), lambda *_: (layer, *zeros), pipeline_mode=pl.Buffered(1))


def _params(n_axes):
    return pltpu.CompilerParams(dimension_semantics=("arbitrary",) * n_axes, vmem_limit_bytes=VMEM_LIMIT)


def _rms(x, w):
    return x * lax.rsqrt(jnp.mean(x * x, axis=-1, keepdims=True) + EPS) * w


def _dot(a, b):
    return jnp.dot(a, b, preferred_element_type=F32)


def _bf16(w_ref):
    return w_ref[...].astype(BF16)


def _in_proj_kernel(x_ref, nw_ref, w_ref, bg_ref, *rest, tiles_per_seq, v_transposed):
    if v_transposed:
        wvt_ref, u_ref, q_ref, k_ref, v_ref, g_ref, kst_ref, vst_ref = rest
    else:
        u_ref, q_ref, k_ref, v_ref, g_ref, kst_ref, vst_ref = rest
    d = D_MODEL
    h = _rms(x_ref[...], nw_ref[...]).astype(BF16)
    glu_a = _dot(h, w_ref[:, 0:d])
    glu_b = _dot(h, w_ref[:, d:2 * d])
    u_ref[...] = glu_a * jax.nn.sigmoid(glu_b)

    def put_pairs(dst_ref, val):
        for hp in range(N_PAIRS):
            dst_ref[hp] = val[:, hp * LANES:(hp + 1) * LANES]

    put_pairs(q_ref, (_dot(h, w_ref[:, 2 * d:3 * d]) * Q_SCALE).astype(BF16))
    k = _dot(h, w_ref[:, 3 * d:4 * d])
    put_pairs(k_ref, k.astype(BF16))
    kst_ref[0] = k
    if v_transposed:
        vt = lax.dot_general(wvt_ref[...], h, (((1,), (1,)), ((), ())), preferred_element_type=F32).astype(BF16)
        for hp in range(N_PAIRS):
            v_ref[hp] = vt[hp * LANES:(hp + 1) * LANES, :]
    else:
        v = _dot(h, w_ref[:, 4 * d:5 * d])
        put_pairs(v_ref, v.astype(BF16))
        vst_ref[0] = v
    g_ref[...] = jax.nn.sigmoid(_dot(h, w_ref[:, 5 * d:7 * d]) + bg_ref[...])
    if v_transposed:
        @pl.when(pl.program_id(0) % tiles_per_seq == tiles_per_seq - 1)
        def _():
            vst_ref[0] = _dot(h, w_ref[:, 4 * d:5 * d])


def _in_proj(x, norm_w, w_in, b_gate, layer, tiles_per_seq, tm=ROW_TILE, w_vt=None):
    t, d = x.shape
    n_tiles = t // tm
    n_seq = n_tiles // tiles_per_seq
    row = lambda i: (i, 0)
    st = lambda i: (i // tiles_per_seq, 0, 0)
    pairs = pl.BlockSpec((N_PAIRS, tm, LANES), lambda i: (0, i, 0))
    pairs_shape = jax.ShapeDtypeStruct((N_PAIRS, t, LANES), BF16)
    vt = w_vt is not None
    v_spec = pl.BlockSpec((N_PAIRS, LANES, tm), lambda i: (0, 0, i)) if vt else pairs
    v_shape = jax.ShapeDtypeStruct((N_PAIRS, LANES, t), BF16) if vt else pairs_shape
    return pl.pallas_call(
        functools.partial(_in_proj_kernel, tiles_per_seq=tiles_per_seq, v_transposed=vt),
        grid=(n_tiles,),
        in_specs=[pl.BlockSpec((tm, d), row), _layer_spec((1, d), layer), _layer_spec((d, 7 * d), layer),
                  _layer_spec((1, 2 * d), layer)] + ([_layer_spec((d, d), layer)] if vt else []),
        out_specs=[pl.BlockSpec((tm, d), row), pairs, pairs, v_spec, pl.BlockSpec((tm, 2 * d), row),
                   pl.BlockSpec((1, tm, d), st), pl.BlockSpec((1, tm, d), st)],
        out_shape=[jax.ShapeDtypeStruct((t, d), F32), pairs_shape, pairs_shape, v_shape,
                   jax.ShapeDtypeStruct((t, 2 * d), F32),
                   jax.ShapeDtypeStruct((n_seq, tm, d), F32), jax.ShapeDtypeStruct((n_seq, tm, d), F32)],
        compiler_params=_params(1),
        name="in_proj",
    )(x, norm_w, w_in, b_gate, *([w_vt] if vt else []))


def _conv_kernel(u_ref, hist_ref, w_ref, b_ref, lg_ref, lb_ref, wo_ref, c_ref, xe_ref, xs_ref, y_ref, *, tm, rows):
    j = pl.program_id(1)

    @pl.when(j == 0)
    def _():
        for cb in range(D_MODEL // LANES):
            xe_ref[cb, 0:CONV_HIST] = hist_ref[:, cb * LANES:(cb + 1) * LANES]

    @pl.when(j > 0)
    def _():
        for cb in range(D_MODEL // LANES):
            xe_ref[cb, 0:CONV_HIST] = xe_ref[cb, tm:tm + CONV_HIST]

    n_shift = CONV_HIST + tm - SUBLANES
    first = CONV_HIST - (CONV_W - 1)
    for cb in range(D_MODEL // LANES):
        cs = slice(cb * LANES, (cb + 1) * LANES)
        xe_ref[cb, CONV_HIST:CONV_HIST + tm] = u_ref[:, cs]
        for r in range(1, SUBLANES):
            xs_ref[cb, r - 1] = xe_ref[cb, r:r + n_shift]

        for r0 in range(0, tm, rows):
            acc = jnp.broadcast_to(b_ref[:, cs], (rows, LANES))
            for tap in range(CONV_W):
                shift = (first + tap) % SUBLANES
                base = first + tap - shift
                if shift == 0:
                    win = xe_ref[cb, r0 + base:r0 + base + rows, :]
                else:
                    win = xs_ref[cb, shift - 1, r0 + base:r0 + base + rows, :]
                acc = acc + w_ref[tap:tap + 1, cs] * win
            y_ref[r0:r0 + rows, cs] = acc

    y = y_ref[...]
    mu = jnp.mean(y, axis=-1, keepdims=True)
    yc = y - mu
    var = jnp.mean(yc * yc, axis=-1, keepdims=True)
    z = yc * lax.rsqrt(var + EPS) * lg_ref[...] + lb_ref[...]
    c_ref[...] = _dot(jax.nn.silu(z).astype(BF16), _bf16(wo_ref))


def _conv_module(u, hist, hist_layer, conv_dw, conv_b, ln_g, ln_b, w_conv_out, layer, n_seq, tm, rows=64):
    t, d = u.shape
    nt = t // (n_seq * tm)
    per_seq_hist = hist.shape[1] > 1
    kern = functools.partial(_conv_kernel, tm=tm, rows=rows)
    row = lambda b, j: (b * nt + j, 0)
    return pl.pallas_call(
        kern,
        grid=(n_seq, nt),
        in_specs=[pl.BlockSpec((tm, d), row),
                  pl.BlockSpec((None, None, CONV_HIST, d), lambda b, j: (hist_layer, b if per_seq_hist else 0, 0, 0)),
                  _layer_spec((CONV_W, d), layer), _layer_spec((1, d), layer), _layer_spec((1, d), layer),
                  _layer_spec((1, d), layer), _layer_spec((d, d), layer)],
        out_specs=pl.BlockSpec((tm, d), row),
        out_shape=jax.ShapeDtypeStruct((t, d), F32),
        scratch_shapes=[pltpu.VMEM((d // LANES, CONV_HIST + tm, LANES), F32),
                        pltpu.VMEM((d // LANES, SUBLANES - 1, CONV_HIST + tm - SUBLANES, LANES), F32),
                        pltpu.VMEM((tm, d), F32)],
        compiler_params=_params(2),
        name="conv_module",
    )(u, hist, conv_dw, conv_b, ln_g, ln_b, w_conv_out)


def _bias_kernel(t_ref, bt_ref, bs_ref):
    t = t_ref[...]
    hi = t.astype(BF16)
    rest = t - hi.astype(F32)
    mid = rest.astype(BF16)
    lo = (rest - mid.astype(F32)).astype(BF16)
    idx = lax.broadcasted_iota(jnp.int32, (TABLE_PAD, K_WINDOW), 0)
    col = lax.broadcasted_iota(jnp.int32, (TABLE_PAD, K_WINDOW), 1)
    sel = jnp.where(col < ATT_PAST + CHUNK, jnp.clip(ATT_PAST - col, -REL_CLIP, REL_CLIP) + REL_CLIP, 2 * REL_CLIP)
    onehot = jnp.where(idx == sel, 1.0, 0.0).astype(BF16)
    row0 = (_dot(hi, onehot) + _dot(mid, onehot) + _dot(lo, onehot)) * LOG2E
    r = lax.broadcasted_iota(jnp.int32, (Q_BLOCK, K_WINDOW), 0)
    c = lax.broadcasted_iota(jnp.int32, (Q_BLOCK, K_WINDOW), 1)
    band = ((r < CHUNK) & (c < ATT_PAST + CHUNK)) | ((r >= CHUNK) & (c >= CHUNK))
    for h in range(N_HEADS):
        x = pltpu.roll(jnp.broadcast_to(row0[h:h + 1], (Q_BLOCK, K_WINDOW)), 0, 1, stride=1, stride_axis=0)
        x = jnp.where(band, x, NEG_INF)
        bt_ref[h // 2, :, (h % 2) * Q_BLOCK:(h % 2 + 1) * Q_BLOCK] = x.T
        bs_ref[h] = x[:CHUNK]


def _attention_bias(rel_table):
    assert ATT_PAST >= REL_CLIP and rel_table.shape[2] == 2 * REL_CLIP + 1
    depth = rel_table.shape[0]
    table = jnp.pad(rel_table, ((0, 0), (0, 0), (0, TABLE_PAD - rel_table.shape[2])))
    bt, bs = pl.pallas_call(
        _bias_kernel,
        grid=(depth,),
        in_specs=[pl.BlockSpec((None, N_HEADS, TABLE_PAD), lambda l: (l, 0, 0))],
        out_specs=[pl.BlockSpec((None, N_PAIRS, K_WINDOW, 2 * Q_BLOCK), lambda l: (l, 0, 0, 0)),
                   pl.BlockSpec((None, N_HEADS, CHUNK, K_WINDOW), lambda l: (l, 0, 0, 0))],
        out_shape=[jax.ShapeDtypeStruct((depth, N_PAIRS, K_WINDOW, 2 * Q_BLOCK), F32),
                   jax.ShapeDtypeStruct((depth, N_HEADS, CHUNK, K_WINDOW), F32)],
        compiler_params=_params(1),
        name="rel_bias",
    )(table)
    return bt, bs.reshape(depth, N_PAIRS, 2 * CHUNK, K_WINDOW)


def _stack_heads(q_blk):
    lane = lax.broadcasted_iota(jnp.int32, q_blk.shape, 1)
    zero = jnp.zeros_like(q_blk)
    return jnp.concatenate([jnp.where(lane < HEAD_DIM, q_blk, zero), jnp.where(lane >= HEAD_DIM, q_blk, zero)], axis=0)


def _scores(q_blk, k_win):
    return lax.dot_general(_stack_heads(q_blk), k_win, (((1,), (1,)), ((), ())), preferred_element_type=F32)


def _softmax_numerator(s):
    half = s.shape[0] // 2
    return jnp.concatenate([jnp.exp2(p - jnp.max(p, axis=-1, keepdims=True)).astype(BF16) for p in (s[:half], s[half:])],
                           axis=0)


def _weighted_values(e, v_win):
    qb = e.shape[0] // 2
    o = _dot(e, jnp.concatenate([v_win, jnp.ones_like(v_win)], axis=1))
    o = o[:, :LANES] / o[:, LANES:]
    lane = lax.broadcasted_iota(jnp.int32, (qb, LANES), 1)
    return jnp.where(lane < HEAD_DIM, o[:qb], o[qb:])


def _merge_residual(x_ref, g_ref, c_ref, att, wao_ref, wout_ref):
    d = D_MODEL
    merged = g_ref[:, 0:d] * c_ref[...] + g_ref[:, d:2 * d] * _dot(att, _bf16(wao_ref))
    return x_ref[...] + _dot(merged.astype(BF16), _bf16(wout_ref))


def _attn_prompt_kernel(q_ref, kp_ref, kc_ref, vp_ref, vc_ref, bm_ref, x_ref, g_ref, c_ref, wao_ref, wout_ref, y_ref,
                        kcat_ref, vcat_ref, s_ref, e_ref, att_ref, *, tm):
    j = pl.program_id(1)
    n_qb = tm // Q_BLOCK
    n_steps = N_PAIRS * n_qb
    for hp in range(N_PAIRS):
        kcat_ref[hp, 0:tm] = kp_ref[hp]
        kcat_ref[hp, tm:2 * tm] = kc_ref[hp]
        vcat_ref[hp, :, 0:tm] = vp_ref[hp]
        vcat_ref[hp, :, tm:2 * tm] = vc_ref[hp]
    key_idx = lax.broadcasted_iota(jnp.int32, (K_WINDOW, 1), 0)
    ones = jnp.ones((2 * SUBLANES, K_WINDOW), BF16)

    def where(n):
        return n // n_qb, (n % n_qb) * Q_BLOCK

    def scores(n):
        hp, r0 = where(n)
        s_ref[n % PIPE] = lax.dot_general(kcat_ref[hp, r0:r0 + K_WINDOW, :], _stack_heads(q_ref[hp, r0:r0 + Q_BLOCK, :]),
                                          (((1,), (1,)), ((), ())), preferred_element_type=F32)

    def softmax(n):
        hp, r0 = where(n)
        key_mask = jnp.where((j == 0) & (key_idx + r0 < tm), NEG_INF, 0.0).astype(F32)
        for half in range(2):
            cols = slice(half * Q_BLOCK, (half + 1) * Q_BLOCK)
            s = s_ref[n % PIPE, :, cols] + bm_ref[hp, :, cols] + key_mask
            e_ref[n % PIPE, :, cols] = jnp.exp2(s - jnp.max(s, axis=0, keepdims=True)).astype(BF16)

    def values(n):
        hp, r0 = where(n)
        vw = jnp.concatenate([vcat_ref[hp, :, r0:r0 + K_WINDOW], ones], axis=0)
        o = _dot(vw, e_ref[n % PIPE])
        row = lax.broadcasted_iota(jnp.int32, (LANES, Q_BLOCK), 0)
        num = jnp.where(row < HEAD_DIM, o[:LANES, :Q_BLOCK], o[:LANES, Q_BLOCK:])
        den = jnp.where(row < HEAD_DIM, o[LANES:LANES + 1, :Q_BLOCK], o[LANES:LANES + 1, Q_BLOCK:])
        att_ref[hp, r0:r0 + Q_BLOCK, :] = (num / den).T

    lag = PIPE // 2
    for n in range(-lag, n_steps + lag):
        for stage, k in ((scores, n + lag), (softmax, n), (values, n - lag)):
            if 0 <= k < n_steps:
                stage(k)
    att = jnp.concatenate([att_ref[hp].astype(BF16) for hp in range(N_PAIRS)], axis=1)
    y_ref[...] = _merge_residual(x_ref, g_ref, c_ref, att, wao_ref, wout_ref)


def _attn_prompt(q, k, v, bias, x, g, c, w_att_out, w_out, layer, n_seq, tm=ATT_PAST):
    _, t, _ = q.shape
    d = D_MODEL
    nt = t // (n_seq * tm)
    cur = lambda b, j: (0, b * nt + j, 0)
    prev = lambda b, j: (0, b * nt + jnp.maximum(j - 1, 0), 0)
    cur_t = lambda b, j: (0, 0, b * nt + j)
    prev_t = lambda b, j: (0, 0, b * nt + jnp.maximum(j - 1, 0))
    row = lambda b, j: (b * nt + j, 0)
    blk = (N_PAIRS, tm, LANES)
    blk_t = (N_PAIRS, LANES, tm)
    return pl.pallas_call(
        functools.partial(_attn_prompt_kernel, tm=tm),
        grid=(n_seq, nt),
        in_specs=[pl.BlockSpec(blk, cur), pl.BlockSpec(blk, prev), pl.BlockSpec(blk, cur),
                  pl.BlockSpec(blk_t, prev_t), pl.BlockSpec(blk_t, cur_t), _layer_spec(bias.shape[1:], layer),
                  pl.BlockSpec((tm, d), row), pl.BlockSpec((tm, 2 * d), row), pl.BlockSpec((tm, d), row),
                  _layer_spec((d, d), layer), _layer_spec((d, d), layer)],
        out_specs=pl.BlockSpec((tm, d), row),
        out_shape=jax.ShapeDtypeStruct((t, d), F32),
        scratch_shapes=[pltpu.VMEM((N_PAIRS, 2 * tm, LANES), BF16), pltpu.VMEM((N_PAIRS, LANES, 2 * tm), BF16),
                        pltpu.VMEM((PIPE, K_WINDOW, 2 * Q_BLOCK), F32), pltpu.VMEM((PIPE, K_WINDOW, 2 * Q_BLOCK), BF16),
                        pltpu.VMEM((N_PAIRS, tm, LANES), F32)],
        compiler_params=_params(2),
        name="attn_prompt",
    )(q, k, k, v, v, bias, x, g, c, w_att_out, w_out)


def _attn_sample_kernel(q_ref, kc_ref, kn_ref, knf_ref, vc_ref, vn_ref, vnf_ref, bm_ref, x_ref, g_ref, c_ref,
                        wao_ref, wout_ref, y_ref, ko_ref, vo_ref, kcat_ref, vcat_ref, *, seqs):
    l, t = ATT_PAST, CHUNK
    pad = jnp.zeros((K_WINDOW - l - t, LANES), BF16)
    att_rows = []
    for s in range(seqs):
        rs = slice(s * t, (s + 1) * t)
        for cache_ref, new_ref, new_f32_ref, out_ref, cat_ref in ((kc_ref, kn_ref, knf_ref, ko_ref, kcat_ref),
                                                                  (vc_ref, vn_ref, vnf_ref, vo_ref, vcat_ref)):
            for hp in range(N_PAIRS):
                cat_ref[hp, 0:l] = cache_ref[s, :, hp * LANES:(hp + 1) * LANES].astype(BF16)
                cat_ref[hp, l:l + t] = new_ref[hp, rs, :]
                cat_ref[hp, l + t:K_WINDOW] = pad
            out_ref[s, 0:l - t] = cache_ref[s, t:l]
            out_ref[s, l - t:l] = new_f32_ref[rs, :]
        att = []
        for hp in range(N_PAIRS):
            e = _softmax_numerator(_scores(q_ref[hp, rs, :], kcat_ref[hp]) + bm_ref[hp])
            att.append(_weighted_values(e, vcat_ref[hp]).astype(BF16))
        att_rows.append(jnp.concatenate(att, axis=1))
    y_ref[...] = _merge_residual(x_ref, g_ref, c_ref, jnp.concatenate(att_rows, axis=0), wao_ref, wout_ref)


def _attn_sample(q, cache_k, k_new, k_new_f32, cache_v, v_new, v_new_f32, bias, x, g, c, w_att_out, w_out, layer):
    _, t, _ = q.shape
    d = D_MODEL
    n_seq = t // CHUNK
    seqs = SAMPLE_SEQS
    assert n_seq % seqs == 0
    rows = seqs * CHUNK
    row = lambda b: (b, 0)
    pairs = pl.BlockSpec((N_PAIRS, rows, LANES), lambda b: (0, b, 0))
    cache = pl.BlockSpec((None, seqs, ATT_PAST, d), lambda b: (layer, b, 0, 0))
    new_cache = pl.BlockSpec((seqs, ATT_PAST, d), lambda b: (b, 0, 0))
    return pl.pallas_call(
        functools.partial(_attn_sample_kernel, seqs=seqs),
        grid=(n_seq // seqs,),
        in_specs=[pairs, cache, pairs, pl.BlockSpec((rows, d), row), cache, pairs, pl.BlockSpec((rows, d), row),
                  _layer_spec(bias.shape[1:], layer),
                  pl.BlockSpec((rows, d), row), pl.BlockSpec((rows, 2 * d), row), pl.BlockSpec((rows, d), row),
                  _layer_spec((d, d), layer), _layer_spec((d, d), layer)],
        out_specs=[pl.BlockSpec((rows, d), row), new_cache, new_cache],
        out_shape=[jax.ShapeDtypeStruct((t, d), F32), jax.ShapeDtypeStruct((n_seq, ATT_PAST, d), F32),
                   jax.ShapeDtypeStruct((n_seq, ATT_PAST, d), F32)],
        scratch_shapes=[pltpu.VMEM((N_PAIRS, K_WINDOW, LANES), BF16), pltpu.VMEM((N_PAIRS, K_WINDOW, LANES), BF16)],
        compiler_params=_params(1),
        name="attn_sample",
    )(q, cache_k, k_new, k_new_f32, cache_v, v_new, v_new_f32, bias, x, g, c, w_att_out, w_out)


def _ffn_kernel(x_ref, p_ref, fh_ref, nf_ref, wup_ref, fw_ref, fb_ref, wdn_ref,
                npl_ref, wpg_ref, wpp_ref, nfin_ref, y_ref, fst_ref, up_ref, *, tm, seg_rows, tiles_per_seq, final):
    i = pl.program_id(0)
    seqs = tm // seg_rows
    stride = FFN_HIST + seg_rows

    if seqs == 1:
        @pl.when(i % tiles_per_seq == 0)
        def _():
            up_ref[0:FFN_HIST] = fh_ref[0]

        @pl.when(i % tiles_per_seq != 0)
        def _():
            up_ref[0:FFN_HIST] = up_ref[tm:tm + FFN_HIST]
    else:
        for s in range(seqs):
            up_ref[s * stride:s * stride + FFN_HIST] = fh_ref[s]

    x = x_ref[...]
    h = _rms(x, nf_ref[...]).astype(BF16)
    up = _dot(h, wup_ref[:, 0:D_FF])
    for s in range(seqs):
        up_ref[s * stride + FFN_HIST:(s + 1) * stride] = up[s * seg_rows:(s + 1) * seg_rows]
    gate = _dot(h, wup_ref[:, D_FF:2 * D_FF])
    cvs = []
    for s in range(seqs):
        fst_ref[s] = up_ref[(s + 1) * stride - FFN_HIST:(s + 1) * stride]
        cv = fb_ref[...]
        for tap in range(FFN_W):
            first = s * stride + FFN_HIST - (FFN_W - 1) + tap
            cv = cv + fw_ref[tap:tap + 1] * up_ref[first:first + seg_rows]
        cvs.append(cv)
    cv = cvs[0] if seqs == 1 else jnp.concatenate(cvs, axis=0)
    act = cv * (lax.erf(cv / jnp.sqrt(F32(2.0))) + 1.0) / 2.0 * gate
    x = x + _dot(act.astype(BF16), _bf16(wdn_ref))

    h = _rms(x, npl_ref[...]).astype(BF16)
    x = x + jax.nn.sigmoid(_dot(h, _bf16(wpg_ref))) * _dot(p_ref[...].astype(BF16), _bf16(wpp_ref))
    y_ref[...] = _rms(x, nfin_ref[...]) if final else x


def _ffn(x, p, ffn_hist, hist_layer, norm_ffn, w_up, ffn_dw, ffn_b, w_down, norm_ple, w_pg, w_pp,
         norm_final, layer, seg, tm, final):
    t, d = x.shape
    n_tiles = t // tm
    seg_rows = min(seg, tm)
    seqs = tm // seg_rows
    tiles_per_seq = seg // seg_rows
    per_seq_hist = ffn_hist.shape[1] > 1
    row = lambda i: (i, 0)
    seq_blk = lambda i: (i // tiles_per_seq, 0, 0)
    kern = functools.partial(_ffn_kernel, tm=tm, seg_rows=seg_rows, tiles_per_seq=tiles_per_seq, final=final)
    return pl.pallas_call(
        kern,
        grid=(n_tiles,),
        in_specs=[pl.BlockSpec((tm, d), row), pl.BlockSpec((None, tm, PLE_DIM), lambda i: (layer, i, 0)),
                  pl.BlockSpec((None, seqs, FFN_HIST, D_FF),
                               lambda i: (hist_layer, (i // tiles_per_seq) if per_seq_hist else 0, 0, 0)),
                  _layer_spec((1, d), layer), _layer_spec((d, 2 * D_FF), layer),
                  _layer_spec((FFN_W, D_FF), layer), _layer_spec((1, D_FF), layer), _layer_spec((D_FF, d), layer),
                  _layer_spec((1, d), layer), _layer_spec((d, d), layer), _layer_spec((PLE_DIM, d), layer),
                  _const_spec((1, d))],
        out_specs=[pl.BlockSpec((tm, d), row), pl.BlockSpec((seqs, FFN_HIST, D_FF), seq_blk)],
        out_shape=[jax.ShapeDtypeStruct((t, d), F32),
                   jax.ShapeDtypeStruct((n_tiles * seqs // tiles_per_seq, FFN_HIST, D_FF), F32)],
        scratch_shapes=[pltpu.VMEM((seqs * (FFN_HIST + seg_rows), D_FF), F32)],
        compiler_params=_params(1),
        name="ffn",
    )(x, p, ffn_hist, norm_ffn, w_up, ffn_dw, ffn_b, w_down, norm_ple, w_pg, w_pp, norm_final)


def _pad_rows_front(h, rows):
    return jnp.pad(h, ((0, 0),) * (h.ndim - 2) + ((rows - h.shape[-2], 0), (0, 0)))


def kernel(x_prompt, x_sample, p_prompt, p_sample, cache_att_k, cache_att_v, state_conv, state_ffn_conv, norm_mix, w_in, conv_dw, conv_dw_b, conv_ln_g, conv_ln_b, w_conv_out, rel_table, w_att_out, b_gate, w_out, norm_ffn, w_ffn_up, ffn_dw, ffn_dw_b, w_ffn_down, norm_ple, w_ple_gate, w_ple_proj, norm_final):
    depth = w_in.shape[0]
    bp, sp, d = x_prompt.shape
    bs, ss, _ = x_sample.shape
    l_cache = cache_att_k.shape[2]
    xp = x_prompt.reshape(bp * sp, d)
    xs = x_sample.reshape(bs * ss, d)
    pp = p_prompt.reshape(depth, bp * sp, PLE_DIM)
    ps = p_sample.reshape(depth, bs * ss, PLE_DIM)
    ck = cache_att_k.reshape(depth, bs, l_cache, d)
    cv = cache_att_v.reshape(depth, bs, l_cache, d)
    rows = lambda v: v.reshape(v.shape[0], 1, v.shape[1])

    w_in_b, w_up_b = (w.astype(BF16) for w in (w_in, w_ffn_up))
    bias_p, bias_s = _attention_bias(rel_table)
    w_vt_b = jnp.swapaxes(w_in[:, :, 4 * d:5 * d], 1, 2).astype(BF16)
    conv_hist_s = _pad_rows_front(state_conv, CONV_HIST)
    ffn_hist_s = _pad_rows_front(state_ffn_conv, FFN_HIST)
    conv_hist_p = jnp.zeros((1, 1, CONV_HIST, d), F32)
    ffn_hist_p = jnp.zeros((1, 1, FFN_HIST, D_FF), F32)
    in_args = (rows(norm_mix), w_in_b, rows(b_gate))
    conv_args = (conv_dw, rows(conv_dw_b), rows(conv_ln_g), rows(conv_ln_b), w_conv_out)
    ffn_args = (rows(norm_ffn), w_up_b, ffn_dw, rows(ffn_dw_b), w_ffn_down, rows(norm_ple), w_ple_gate, w_ple_proj,
                norm_final.reshape(1, d))
    keep = min(ATT_PAST, sp)

    outs = [[] for _ in range(8)]
    for i in range(depth):
        final = i == depth - 1
        u, q, k, v, g, kst, vst = _in_proj(xp, *in_args, layer=i, tiles_per_seq=sp // ROW_TILE, w_vt=w_vt_b)
        c = _conv_module(u, conv_hist_p, 0, *conv_args, layer=i, n_seq=bp, tm=ROW_TILE)
        xp = _attn_prompt(q, k, v, bias_p, xp, g, c, w_att_out, w_out, layer=i, n_seq=bp)
        xp, fst = _ffn(xp, pp, ffn_hist_p, 0, *ffn_args, layer=i, seg=sp, tm=ROW_TILE, final=final)
        outs[0].append(kst[:, ROW_TILE - keep:])
        outs[1].append(vst[:, ROW_TILE - keep:])
        outs[2].append(u.reshape(bp, sp, d)[:, sp - (CONV_W - 1):])
        outs[3].append(fst[:, FFN_HIST - (FFN_W - 1):])

        u, q, k, v, g, kst, vst = _in_proj(xs, *in_args, layer=i, tiles_per_seq=1)
        c = _conv_module(u, conv_hist_s, i, *conv_args, layer=i, n_seq=bs, tm=ss)
        xs, k_cache, v_cache = _attn_sample(q, ck, k, kst[0], cv, v, vst[0], bias_s, xs, g, c, w_att_out, w_out, layer=i)
        xs, fst = _ffn(xs, ps, ffn_hist_s, i, *ffn_args, layer=i, seg=ss, tm=bs * ss, final=final)
        outs[4].append(k_cache)
        outs[5].append(v_cache)
        outs[6].append(jnp.concatenate([state_conv[i], u.reshape(bs, ss, d)], axis=1)[:, ss:])
        outs[7].append(fst[:, FFN_HIST - (FFN_W - 1):])

    heads = lambda o: jnp.stack(o).reshape(depth, -1, o[0].shape[1], N_HEADS, HEAD_DIM)
    return (xp.reshape(bp, sp, d), xs.reshape(bs, ss, d), heads(outs[0]), heads(outs[1]), jnp.stack(outs[2]),
            jnp.stack(outs[3]), heads(outs[4]), heads(outs[5]), jnp.stack(outs[6]), jnp.stack(outs[7]))
```
